```python
import math
import jax, jax.numpy as jnp
from jax import lax
import numpy as np

D_MODEL = 1024
BATCH = 8
SEQ = 4096
DEPTH = 4

N_A_LAYERS = DEPTH // 2
N_B_LAYERS = DEPTH - N_A_LAYERS
D_FF = 2816
GMLP_EXPAND = 6
D_GMLP = GMLP_EXPAND * D_MODEL
D_GATE = D_GMLP // 2
SGU_GROUPS = 8
SGU_GROUP_DIM = D_GATE // SGU_GROUPS
CHUNK = 128
N_HEADS = 8
HEAD_DIM = D_MODEL // N_HEADS // 2
V_DIM = 2 * HEAD_DIM
Q_WIDTH = N_HEADS * 2 * HEAD_DIM
K_WIDTH = N_HEADS * 2 * HEAD_DIM
KV_WIDTH = K_WIDTH + N_HEADS * V_DIM
Q_BLOCK = 128
RMS_EPS = 1e-6
LN_EPS = 1e-5

kernel_name = "yoco_gmlp_diffattn_macaron_trunk"


def rmsnorm(x, g):
    xf = x.astype(jnp.float32)
    y = xf * lax.rsqrt(jnp.mean(xf * xf, axis=-1, keepdims=True) + RMS_EPS)
    return (y * g.astype(jnp.float32)).astype(x.dtype)


def layernorm(x, g, b):
    xf = x.astype(jnp.float32)
    mu = jnp.mean(xf, axis=-1, keepdims=True)
    var = jnp.mean(jnp.square(xf - mu), axis=-1, keepdims=True)
    y = (xf - mu) * lax.rsqrt(var + LN_EPS)
    return (y * g.astype(jnp.float32) + b.astype(jnp.float32)).astype(x.dtype)


def swiglu_ffn(x, w_gate_up, w_down):
    gate, up = jnp.split(x @ w_gate_up, 2, axis=-1)
    return (jax.nn.silu(gate) * up) @ w_down


def chunked_sgu_mixer(h, w_in, b_in, ln_g, ln_b, w_s, b_s, w_out, b_out):
    B, S, _ = h.shape
    z = jax.nn.gelu(h @ w_in + b_in, approximate=False)
    u, v = jnp.split(z, 2, axis=-1)
    v = layernorm(v, ln_g, ln_b)
    v = v.reshape(B, S // CHUNK, CHUNK, SGU_GROUPS, SGU_GROUP_DIM)
    causal = jnp.tril(jnp.ones((CHUNK, CHUNK), dtype=bool))
    ws = jnp.where(causal[None], w_s, jnp.zeros((), w_s.dtype))
    s = jnp.einsum('gts,bnsgc->bntgc', ws, v) + b_s.T[:, :, None]
    s = s.reshape(B, S, D_GATE)
    return (u * s) @ w_out + b_out


def shared_kv(h_kv, w_kv):
    B, S, _ = h_kv.shape
    kv = h_kv @ w_kv
    k = kv[..., :K_WIDTH].reshape(B, S, N_HEADS, 2, HEAD_DIM)
    v = kv[..., K_WIDTH:].reshape(B, S, N_HEADS, V_DIM)
    return k, v


def diff_attention(h, k, v, w_q, lam_params, subln_g, w_o, lambda_init):
    B, S, _ = h.shape
    n_blocks = S // Q_BLOCK
    q = (h @ w_q).reshape(B, n_blocks, Q_BLOCK, N_HEADS, 2, HEAD_DIM)
    q = jnp.moveaxis(q, 1, 0)
    lp = lam_params.astype(jnp.float32)
    lam = (jnp.exp(jnp.sum(lp[0] * lp[1])) - jnp.exp(jnp.sum(lp[2] * lp[3]))
           + lambda_init)
    scale = HEAD_DIM ** -0.5
    kpos = jnp.arange(S)
    neg = jnp.finfo(jnp.float32).min

    def one_block(args):
        qb, bi = args
        scores = jnp.einsum('bqhcd,bkhcd->bhcqk', qb, k).astype(jnp.float32) * scale
        qpos = bi * Q_BLOCK + jnp.arange(Q_BLOCK)
        mask = kpos[None, :] <= qpos[:, None]
        p = jax.nn.softmax(jnp.where(mask, scores, neg), axis=-1)
        a = p[:, :, 0] - lam * p[:, :, 1]
        return jnp.einsum('bhqk,bkhe->bqhe', a.astype(v.dtype), v)

    o = lax.map(one_block, (q, jnp.arange(n_blocks)))
    o = jnp.moveaxis(o, 0, 1).reshape(B, S, N_HEADS, V_DIM)
    o = rmsnorm(o, subln_g) * (1.0 - lambda_init)
    return o.reshape(B, S, N_HEADS * V_DIM) @ w_o


def setup_inputs(seed: int = 0) -> dict:
    key = jax.random.key(seed)
    ks = jax.random.split(key, 20)
    f32 = jnp.float32

    def nrm(k, shape, scale):
        return jax.random.normal(k, shape, f32) * scale

    return {
        "x": nrm(ks[0], (BATCH, SEQ, D_MODEL), 1.0),
        "norm_g": 1.0 + nrm(ks[1], (DEPTH, 3, D_MODEL), 0.02),
        "ffn_w_gate_up": nrm(ks[2], (DEPTH, 2, D_MODEL, 2 * D_FF), D_MODEL ** -0.5),
        "ffn_w_down": nrm(ks[3], (DEPTH, 2, D_FF, D_MODEL), D_FF ** -0.5),
        "a_w_in": nrm(ks[4], (N_A_LAYERS, D_MODEL, D_GMLP), D_MODEL ** -0.5),
        "a_b_in": nrm(ks[5], (N_A_LAYERS, D_GMLP), 0.02),
        "a_ln_g": 1.0 + nrm(ks[6], (N_A_LAYERS, D_GATE), 0.02),
        "a_ln_b": nrm(ks[7], (N_A_LAYERS, D_GATE), 0.02),
        "a_w_s": nrm(ks[8], (N_A_LAYERS, SGU_GROUPS, CHUNK, CHUNK), CHUNK ** -0.5),
        "a_b_s": 1.0 + nrm(ks[9], (N_A_LAYERS, SGU_GROUPS, CHUNK), 0.02),
        "a_w_out": nrm(ks[10], (N_A_LAYERS, D_GATE, D_MODEL), D_GATE ** -0.5),
        "a_b_out": nrm(ks[11], (N_A_LAYERS, D_MODEL), 0.02),
        "kv_norm_g": 1.0 + nrm(ks[12], (D_MODEL,), 0.02),
        "w_kv": nrm(ks[13], (D_MODEL, KV_WIDTH), D_MODEL ** -0.5),
        "b_w_q": nrm(ks[14], (N_B_LAYERS, D_MODEL, Q_WIDTH), D_MODEL ** -0.5),
        "b_lambda": nrm(ks[15], (N_B_LAYERS, 4, HEAD_DIM), 0.1),
        "b_subln_g": 1.0 + nrm(ks[16], (N_B_LAYERS, V_DIM), 0.02),
        "b_w_o": nrm(ks[17], (N_B_LAYERS, N_HEADS * V_DIM, D_MODEL), (N_HEADS * V_DIM) ** -0.5),
        "final_norm_g": 1.0 + nrm(ks[18], (D_MODEL,), 0.02),
    }


def reference(x, norm_g, ffn_w_gate_up, ffn_w_down, a_w_in, a_b_in, a_ln_g, a_ln_b,
              a_w_s, a_b_s, a_w_out, a_b_out, kv_norm_g, w_kv, b_w_q, b_lambda,
              b_subln_g, b_w_o, final_norm_g):
    h = x
    k_shared = None
    v_shared = None
    for layer in range(DEPTH):
        h = h + 0.5 * swiglu_ffn(rmsnorm(h, norm_g[layer, 0]),
                                 ffn_w_gate_up[layer, 0], ffn_w_down[layer, 0])
        hn = rmsnorm(h, norm_g[layer, 1])
        if layer < N_A_LAYERS:
            i = layer
            h = h + chunked_sgu_mixer(hn, a_w_in[i], a_b_in[i], a_ln_g[i], a_ln_b[i],
                                      a_w_s[i], a_b_s[i], a_w_out[i], a_b_out[i])
        else:
            j = layer - N_A_LAYERS
            lambda_init = 0.8 - 0.6 * math.exp(-0.3 * layer)
            h = h + diff_attention(hn, k_shared, v_shared, b_w_q[j], b_lambda[j],
                                   b_subln_g[j], b_w_o[j], lambda_init)
        h = h + 0.5 * swiglu_ffn(rmsnorm(h, norm_g[layer, 2]),
                                 ffn_w_gate_up[layer, 1], ffn_w_down[layer, 1])
        if layer == N_A_LAYERS - 1:
            k_shared, v_shared = shared_kv(rmsnorm(h, kv_norm_g), w_kv)
    return rmsnorm(h, final_norm_g)
```

```python
import functools
import math

import jax
import jax.numpy as jnp
from jax import lax
from jax.experimental import pallas as pl
from jax.experimental.pallas import tpu as pltpu

D_MODEL = 1024
DEPTH = 4
N_A_LAYERS = DEPTH // 2
D_FF = 2816
D_GMLP = 6 * D_MODEL
D_GATE = D_GMLP // 2
SGU_GROUPS = 8
SGU_GROUP_DIM = D_GATE // SGU_GROUPS
CHUNK = 128
N_HEADS = 8
HEAD_DIM = 64
V_DIM = 128
K_WIDTH = N_HEADS * 2 * HEAD_DIM
RMS_EPS = 1e-6
LN_EPS = 1e-5

BF16 = jnp.bfloat16
F32 = jnp.float32

VMEM_LIMIT_BYTES = 56 * 1024 * 1024

FFN_ROWS = 512
FFN_COLS = 256
GMLP_ROWS = 256
PROJ_ROWS = 512
ATTN_BLOCK = 512


def _resident(shape):
    return pl.BlockSpec(shape, lambda *_: (0,) * len(shape),
                        pipeline_mode=pl.Buffered(1))


def _params():
    return pltpu.CompilerParams(vmem_limit_bytes=VMEM_LIMIT_BYTES)


def _rms(x, g):
    return x * lax.rsqrt(jnp.mean(x * x, axis=-1, keepdims=True) + RMS_EPS) * g


def _gelu(z):
    return 0.5 * z * (1.0 + lax.erf(z * (2.0 ** -0.5)))


def _dot(a, b):
    return jnp.dot(a, b, preferred_element_type=F32)


def _ffn_kernel(h_ref, g_ref, wgu_ref, wd_ref, fg_ref, o_ref, act_ref, *, final_norm):
    h = h_ref[...]
    xn = _rms(h, g_ref[...]).astype(BF16)
    for c in range(D_FF // FFN_COLS):
        lo = c * FFN_COLS
        gate = _dot(xn, wgu_ref[:, lo:lo + FFN_COLS])
        up = _dot(xn, wgu_ref[:, D_FF + lo:D_FF + lo + FFN_COLS])
        act = gate * (1.0 / (1.0 + jnp.exp(-gate))) * up
        act_ref[:, lo:lo + FFN_COLS] = act.astype(BF16)
    out = h + 0.5 * _dot(act_ref[...], wd_ref[...])
    if final_norm:
        out = _rms(out, fg_ref[...])
    o_ref[...] = out


def _ffn(h, g, wgu, wd, fg, final_norm):
    t = h.shape[0]
    row = pl.BlockSpec((FFN_ROWS, D_MODEL), lambda i: (i, 0))
    return pl.pallas_call(
        functools.partial(_ffn_kernel, final_norm=final_norm),
        grid=(t // FFN_ROWS,),
        in_specs=[row, _resident((1, D_MODEL)), _resident((D_MODEL, 2 * D_FF)),
                  _resident((D_FF, D_MODEL)), _resident((1, D_MODEL))],
        out_specs=row,
        out_shape=jax.ShapeDtypeStruct((t, D_MODEL), F32),
        scratch_shapes=[pltpu.VMEM((FFN_ROWS, D_FF), BF16)],
        compiler_params=_params(),
        name="ffn",
    )(h, g, wgu, wd, fg)


def _gmlp_kernel(h_ref, g_ref, win_ref, bin_ref, lng_ref, lnb_ref, ws_ref, bst_ref,
                 wout_ref, bout_ref, o_ref, v_ref, y_ref):
    h = h_ref[...]
    xn = _rms(h, g_ref[...]).astype(BF16)
    gd = SGU_GROUP_DIM

    s1 = jnp.zeros((GMLP_ROWS, 1), F32)
    s2 = jnp.zeros((GMLP_ROWS, 1), F32)
    for g in range(SGU_GROUPS):
        lo = D_GATE + g * gd
        v = _gelu(_dot(xn, win_ref[:, lo:lo + gd]) + bin_ref[:, lo:lo + gd])
        v_ref[:, g * gd:(g + 1) * gd] = v
        s1 = s1 + jnp.sum(v, axis=-1, keepdims=True)
        s2 = s2 + jnp.sum(v * v, axis=-1, keepdims=True)
    mu = s1 * (1.0 / D_GATE)
    var = s2 * (1.0 / D_GATE) - mu * mu
    rstd = lax.rsqrt(var + LN_EPS)

    rows = lax.broadcasted_iota(jnp.int32, (CHUNK, CHUNK), 0)
    cols = lax.broadcasted_iota(jnp.int32, (CHUNK, CHUNK), 1)
    causal = rows >= cols

    for g in range(SGU_GROUPS):
        lo = g * gd
        vn = ((v_ref[:, lo:lo + gd] - mu) * rstd * lng_ref[:, lo:lo + gd]
              + lnb_ref[:, lo:lo + gd]).astype(BF16)
        ws = jnp.where(causal, ws_ref[g], 0.0).astype(BF16)
        bias = bst_ref[:, g:g + 1]
        s = jnp.concatenate(
            [_dot(ws, vn[c * CHUNK:(c + 1) * CHUNK]) + bias
             for c in range(GMLP_ROWS // CHUNK)], axis=0)
        u = _gelu(_dot(xn, win_ref[:, lo:lo + gd]) + bin_ref[:, lo:lo + gd])
        y_ref[:, lo:lo + gd] = (u * s).astype(BF16)

    o_ref[...] = h + _dot(y_ref[...], wout_ref[...]) + bout_ref[...]


def _gmlp(h, g, win, b_in, ln_g, ln_b, ws, bst, wout, b_out):
    t = h.shape[0]
    row = pl.BlockSpec((GMLP_ROWS, D_MODEL), lambda i: (i, 0))
    return pl.pallas_call(
        _gmlp_kernel,
        grid=(t // GMLP_ROWS,),
        in_specs=[row, _resident((1, D_MODEL)), _resident((D_MODEL, D_GMLP)),
                  _resident((1, D_GMLP)), _resident((1, D_GATE)), _resident((1, D_GATE)),
                  _resident((SGU_GROUPS, CHUNK, CHUNK)), _resident((CHUNK, SGU_GROUPS)),
                  _resident((D_GATE, D_MODEL)), _resident((1, D_MODEL))],
        out_specs=row,
        out_shape=jax.ShapeDtypeStruct((t, D_MODEL), F32),
        scratch_shapes=[pltpu.VMEM((GMLP_ROWS, D_GATE), F32),
                        pltpu.VMEM((GMLP_ROWS, D_GATE), BF16)],
        compiler_params=_params(),
        name="gmlp",
    )(h, g, win, b_in, ln_g, ln_b, ws, bst, wout, b_out)


def _kv_kernel(h_ref, g_ref, wkt_ref, wv_ref, kt_ref, v_ref):
    xn = _rms(h_ref[...], g_ref[...]).astype(BF16)
    kt = lax.dot_general(wkt_ref[...], xn, (((1,), (1,)), ((), ())),
                         preferred_element_type=F32)
    kt_ref[...] = kt.astype(BF16)
    v_ref[...] = _dot(xn, wv_ref[...]).astype(BF16)


def _kv(h, g, wkt, wv, batch, seq):
    n = seq // PROJ_ROWS
    row = pl.BlockSpec((PROJ_ROWS, D_MODEL), lambda b, i: (b * n + i, 0))
    return pl.pallas_call(
        _kv_kernel,
        grid=(batch, n),
        in_specs=[row, _resident((1, D_MODEL)), _resident((K_WIDTH, D_MODEL)),
                  _resident((D_MODEL, N_HEADS * V_DIM))],
        out_specs=[pl.BlockSpec((None, K_WIDTH, PROJ_ROWS), lambda b, i: (b, 0, i)), row],
        out_shape=[jax.ShapeDtypeStruct((batch, K_WIDTH, seq), BF16),
                   jax.ShapeDtypeStruct((batch * seq, N_HEADS * V_DIM), BF16)],
        compiler_params=_params(),
        name="kv",
    )(h, g, wkt, wv)


def _q_kernel(h_ref, g_ref, wq_ref, q_ref):
    xn = _rms(h_ref[...], g_ref[...]).astype(BF16)
    q_ref[...] = (_dot(xn, wq_ref[...]) * (HEAD_DIM ** -0.5)).astype(BF16)


def _qproj(h, g, wq):
    t = h.shape[0]
    row = pl.BlockSpec((PROJ_ROWS, D_MODEL), lambda i: (i, 0))
    return pl.pallas_call(
        _q_kernel,
        grid=(t // PROJ_ROWS,),
        in_specs=[row, _resident((1, D_MODEL)), _resident((D_MODEL, D_MODEL))],
        out_specs=row,
        out_shape=jax.ShapeDtypeStruct((t, D_MODEL), BF16),
        compiler_params=_params(),
        name="qproj",
    )(h, g, wq)


def _attn_kernel(q_ref, kt_ref, v_ref, lam_ref, sg_ref, o_ref,
                 m1_ref, l1_ref, a1_ref, m2_ref, l2_ref, a2_ref, *, lambda_init):
    blk = ATTN_BLOCK
    i = pl.program_id(2)
    q = q_ref[...]
    q1 = q[:, :HEAD_DIM]
    q2 = q[:, HEAD_DIM:]

    for m_ref, l_ref, a_ref in ((m1_ref, l1_ref, a1_ref), (m2_ref, l2_ref, a2_ref)):
        m_ref[...] = jnp.full(m_ref.shape, -jnp.inf, F32)
        l_ref[...] = jnp.zeros(l_ref.shape, F32)
        a_ref[...] = jnp.zeros(a_ref.shape, F32)

    def update(s, vv, m_ref, l_ref, a_ref):
        m_prev = m_ref[...]
        m_next = jnp.maximum(m_prev, jnp.max(s, axis=-1, keepdims=True))
        p = jnp.exp(s - m_next[:, :1])
        alpha = jnp.exp(m_prev - m_next)
        l_ref[...] = alpha * l_ref[...] + jnp.sum(p, axis=-1, keepdims=True)
        a_ref[...] = alpha * a_ref[...] + _dot(p.astype(BF16), vv)
        m_ref[...] = m_next

    def step(j, masked):
        start = pl.multiple_of(j * blk, blk)
        kt = kt_ref[:, pl.ds(start, blk)]
        vv = v_ref[pl.ds(start, blk), :]
        s1 = _dot(q1, kt[:HEAD_DIM])
        s2 = _dot(q2, kt[HEAD_DIM:])
        if masked:
            rows = lax.broadcasted_iota(jnp.int32, (blk, blk), 0)
            cols = lax.broadcasted_iota(jnp.int32, (blk, blk), 1)
            keep = rows >= cols
            s1 = jnp.where(keep, s1, -jnp.inf)
            s2 = jnp.where(keep, s2, -jnp.inf)
        update(s1, vv, m1_ref, l1_ref, a1_ref)
        update(s2, vv, m2_ref, l2_ref, a2_ref)

    def body(j, carry):
        step(j, masked=False)
        return carry

    lax.fori_loop(0, i, body, 0)
    step(i, masked=True)

    lp = lam_ref[...]
    lam = (jnp.exp(jnp.sum(lp[0:1] * lp[1:2], axis=-1, keepdims=True))
           - jnp.exp(jnp.sum(lp[2:3] * lp[3:4], axis=-1, keepdims=True))
           + lambda_init)
    o = a1_ref[...] / l1_ref[...] - lam * (a2_ref[...] / l2_ref[...])
    o = _rms(o, sg_ref[...]) * (1.0 - lambda_init)
    o_ref[...] = o.astype(BF16)


def _attn(q, kt, v, lam_params, subln_g, lambda_init, batch, seq):
    blk = ATTN_BLOCK
    nq = seq // blk
    qo = pl.BlockSpec((blk, V_DIM), lambda b, h, i: (b * nq + i, h))
    stat = pltpu.VMEM((blk, V_DIM), F32)
    return pl.pallas_call(
        functools.partial(_attn_kernel, lambda_init=lambda_init),
        grid=(batch, N_HEADS, nq),
        in_specs=[qo,
                  pl.BlockSpec((None, 2 * HEAD_DIM, seq), lambda b, h, i: (b, h, 0)),
                  pl.BlockSpec((seq, V_DIM), lambda b, h, i: (b, h)),
                  _resident((4, HEAD_DIM)), _resident((1, V_DIM))],
        out_specs=qo,
        out_shape=jax.ShapeDtypeStruct((batch * seq, N_HEADS * V_DIM), BF16),
        scratch_shapes=[stat] * 6,
        compiler_params=_params(),
        name="attn",
    )(q, kt, v, lam_params, subln_g)


def _o_kernel(h_ref, o_ref, wo_ref, out_ref):
    out_ref[...] = h_ref[...] + _dot(o_ref[...], wo_ref[...])


def _oproj(h, o, wo):
    t = h.shape[0]
    row = pl.BlockSpec((PROJ_ROWS, D_MODEL), lambda i: (i, 0))
    return pl.pallas_call(
        _o_kernel,
        grid=(t // PROJ_ROWS,),
        in_specs=[row, row, _resident((N_HEADS * V_DIM, D_MODEL))],
        out_specs=row,
        out_shape=jax.ShapeDtypeStruct((t, D_MODEL), F32),
        compiler_params=_params(),
        name="oproj",
    )(h, o, wo)


def kernel(x, norm_g, ffn_w_gate_up, ffn_w_down, a_w_in, a_b_in, a_ln_g, a_ln_b, a_w_s, a_b_s, a_w_out, a_b_out, kv_norm_g, w_kv, b_w_q, b_lambda, b_subln_g, b_w_o, final_norm_g):
    batch, seq, d = x.shape
    h = x.reshape(batch * seq, d)
    fg = final_norm_g.reshape(1, d)
    kt = v = None
    for layer in range(DEPTH):
        h = _ffn(h, norm_g[layer, 0].reshape(1, d), ffn_w_gate_up[layer, 0].astype(BF16),
                 ffn_w_down[layer, 0].astype(BF16), fg, final_norm=False)
        g_mix = norm_g[layer, 1].reshape(1, d)
        if layer < N_A_LAYERS:
            a = layer
            h = _gmlp(h, g_mix, a_w_in[a].astype(BF16), a_b_in[a].reshape(1, -1),
                      a_ln_g[a].reshape(1, -1), a_ln_b[a].reshape(1, -1), a_w_s[a],
                      a_b_s[a].T, a_w_out[a].astype(BF16), a_b_out[a].reshape(1, -1))
        else:
            j = layer - N_A_LAYERS
            lambda_init = 0.8 - 0.6 * math.exp(-0.3 * layer)
            q = _qproj(h, g_mix, b_w_q[j].astype(BF16))
            o = _attn(q, kt, v, b_lambda[j], b_subln_g[j].reshape(1, -1), lambda_init,
                      batch, seq)
            h = _oproj(h, o, b_w_o[j].astype(BF16))
        h = _ffn(h, norm_g[layer, 2].reshape(1, d), ffn_w_gate_up[layer, 1].astype(BF16),
                 ffn_w_down[layer, 1].astype(BF16), fg, final_norm=(layer == DEPTH - 1))
        if layer == N_A_LAYERS - 1:
            kt, v = _kv(h, kv_norm_g.reshape(1, d), w_kv[:, :K_WIDTH].T.astype(BF16),
                        w_kv[:, K_WIDTH:].astype(BF16), batch, seq)
    return h.reshape(batch, seq, d)
```

```python
import functools
import math

import jax
import jax.numpy as jnp
from jax import lax
from jax.experimental import pallas as pl
from jax.experimental.pallas import tpu as pltpu

D_MODEL = 1024
DEPTH = 4
N_A_LAYERS = DEPTH // 2
D_FF = 2816
D_GMLP = 6 * D_MODEL
D_GATE = D_GMLP // 2
SGU_GROUPS = 8
SGU_GROUP_DIM = D_GATE // SGU_GROUPS
CHUNK = 128
N_HEADS = 8
HEAD_DIM = 64
V_DIM = 128
K_WIDTH = N_HEADS * 2 * HEAD_DIM
RMS_EPS = 1e-6
LN_EPS = 1e-5

BF16 = jnp.bfloat16
F32 = jnp.float32

VMEM_LIMIT_BYTES = 56 * 1024 * 1024

FFN_ROWS = 512
FFN_COLS = 256
GMLP_ROWS = 256
GMLP_COLS = 768
PROJ_ROWS = 512
ATTN_BLOCK = 512
ATTN_HEADS = 2
ONES_ROWS = 16
VT_ROWS = V_DIM + ONES_ROWS
Q_SCALE = HEAD_DIM ** -0.5 * math.log2(math.e)


def _resident(shape):
    return pl.BlockSpec(shape, lambda *_: (0,) * len(shape),
                        pipeline_mode=pl.Buffered(1))


def _params():
    return pltpu.CompilerParams(vmem_limit_bytes=VMEM_LIMIT_BYTES)


def _rms(x, g):
    return x * lax.rsqrt(jnp.mean(x * x, axis=-1, keepdims=True) + RMS_EPS) * g


def _gelu(z):
    return 0.5 * z * (1.0 + lax.erf(z * (2.0 ** -0.5)))


def _dot(a, b):
    return jnp.dot(a, b, preferred_element_type=F32)


def _ffn_kernel(h_ref, g_ref, wgu_ref, wd_ref, fg_ref, o_ref, act_ref, *, final_norm):
    h = h_ref[...]
    xn = _rms(h, g_ref[...]).astype(BF16)
    for c in range(D_FF // FFN_COLS):
        lo = c * FFN_COLS
        gate = _dot(xn, wgu_ref[:, lo:lo + FFN_COLS])
        up = _dot(xn, wgu_ref[:, D_FF + lo:D_FF + lo + FFN_COLS])
        act = gate * (1.0 / (1.0 + jnp.exp(-gate))) * up
        act_ref[:, lo:lo + FFN_COLS] = act.astype(BF16)
    out = h + 0.5 * _dot(act_ref[...], wd_ref[...])
    if final_norm:
        out = _rms(out, fg_ref[...])
    o_ref[...] = out


def _ffn(h, g, wgu, wd, fg, final_norm):
    t = h.shape[0]
    row = pl.BlockSpec((FFN_ROWS, D_MODEL), lambda i: (i, 0))
    return pl.pallas_call(
        functools.partial(_ffn_kernel, final_norm=final_norm),
        grid=(t // FFN_ROWS,),
        in_specs=[row, _resident((1, D_MODEL)), _resident((D_MODEL, 2 * D_FF)),
                  _resident((D_FF, D_MODEL)), _resident((1, D_MODEL))],
        out_specs=row,
        out_shape=jax.ShapeDtypeStruct((t, D_MODEL), F32),
        scratch_shapes=[pltpu.VMEM((FFN_ROWS, D_FF), BF16)],
        compiler_params=_params(),
        name="ffn",
    )(h, g, wgu, wd, fg)


def _gmlp_kernel(h_ref, g_ref, win_ref, bin_ref, lng_ref, lnb_ref, ws_ref, bst_ref,
                 wout_ref, bout_ref, o_ref, v_ref, y_ref):
    h = h_ref[...]
    xn = _rms(h, g_ref[...]).astype(BF16)
    gd = SGU_GROUP_DIM
    cw = GMLP_COLS

    s1 = jnp.zeros((GMLP_ROWS, 1), F32)
    s2 = jnp.zeros((GMLP_ROWS, 1), F32)
    for c in range(D_GATE // cw):
        lo = D_GATE + c * cw
        v = _gelu(_dot(xn, win_ref[:, lo:lo + cw]) + bin_ref[:, lo:lo + cw])
        v_ref[:, c * cw:(c + 1) * cw] = v
        s1 = s1 + jnp.sum(v, axis=-1, keepdims=True)
        s2 = s2 + jnp.sum(v * v, axis=-1, keepdims=True)
    mu = s1 * (1.0 / D_GATE)
    var = s2 * (1.0 / D_GATE) - mu * mu
    rstd = lax.rsqrt(var + LN_EPS)

    rows = lax.broadcasted_iota(jnp.int32, (CHUNK, CHUNK), 0)
    cols = lax.broadcasted_iota(jnp.int32, (CHUNK, CHUNK), 1)
    causal = rows >= cols

    for c in range(D_GATE // cw):
        lo = c * cw
        u = _gelu(_dot(xn, win_ref[:, lo:lo + cw]) + bin_ref[:, lo:lo + cw])
        for g in range(c * (cw // gd), (c + 1) * (cw // gd)):
            glo = g * gd
            vn = ((v_ref[:, glo:glo + gd] - mu) * rstd * lng_ref[:, glo:glo + gd]
                  + lnb_ref[:, glo:glo + gd]).astype(BF16)
            ws = jnp.where(causal, ws_ref[g], 0.0).astype(BF16)
            bias = bst_ref[:, g:g + 1]
            s = jnp.concatenate(
                [_dot(ws, vn[r * CHUNK:(r + 1) * CHUNK]) + bias
                 for r in range(GMLP_ROWS // CHUNK)], axis=0)
            y_ref[:, glo:glo + gd] = (u[:, glo - lo:glo - lo + gd] * s).astype(BF16)

    o_ref[...] = h + _dot(y_ref[...], wout_ref[...]) + bout_ref[...]


def _gmlp(h, g, win, b_in, ln_g, ln_b, ws, bst, wout, b_out):
    t = h.shape[0]
    row = pl.BlockSpec((GMLP_ROWS, D_MODEL), lambda i: (i, 0))
    return pl.pallas_call(
        _gmlp_kernel,
        grid=(t // GMLP_ROWS,),
        in_specs=[row, _resident((1, D_MODEL)), _resident((D_MODEL, D_GMLP)),
                  _resident((1, D_GMLP)), _resident((1, D_GATE)), _resident((1, D_GATE)),
                  _resident((SGU_GROUPS, CHUNK, CHUNK)), _resident((CHUNK, SGU_GROUPS)),
                  _resident((D_GATE, D_MODEL)), _resident((1, D_MODEL))],
        out_specs=row,
        out_shape=jax.ShapeDtypeStruct((t, D_MODEL), F32),
        scratch_shapes=[pltpu.VMEM((GMLP_ROWS, D_GATE), F32),
                        pltpu.VMEM((GMLP_ROWS, D_GATE), BF16)],
        compiler_params=_params(),
        name="gmlp",
    )(h, g, win, b_in, ln_g, ln_b, ws, bst, wout, b_out)


def _kv_kernel(h_ref, g_ref, wk_ref, wvt_ref, k_ref, vt_ref):
    xn = _rms(h_ref[...], g_ref[...]).astype(BF16)
    k_ref[...] = _dot(xn, wk_ref[...]).astype(BF16)
    vt = lax.dot_general(wvt_ref[...], xn, (((1,), (1,)), ((), ())),
                         preferred_element_type=F32).astype(BF16)
    ones = jnp.ones((ONES_ROWS, PROJ_ROWS), BF16)
    for hd in range(N_HEADS):
        vt_ref[hd * VT_ROWS:hd * VT_ROWS + V_DIM, :] = vt[hd * V_DIM:(hd + 1) * V_DIM]
        vt_ref[hd * VT_ROWS + V_DIM:(hd + 1) * VT_ROWS, :] = ones


def _kv(h, g, wk, wvt, batch, seq):
    n = seq // PROJ_ROWS
    row = pl.BlockSpec((PROJ_ROWS, D_MODEL), lambda b, i: (b * n + i, 0))
    return pl.pallas_call(
        _kv_kernel,
        grid=(batch, n),
        in_specs=[row, _resident((1, D_MODEL)), _resident((D_MODEL, K_WIDTH)),
                  _resident((N_HEADS * V_DIM, D_MODEL))],
        out_specs=[row, pl.BlockSpec((None, N_HEADS * VT_ROWS, PROJ_ROWS),
                                     lambda b, i: (b, 0, i))],
        out_shape=[jax.ShapeDtypeStruct((batch * seq, K_WIDTH), BF16),
                   jax.ShapeDtypeStruct((batch, N_HEADS * VT_ROWS, seq), BF16)],
        compiler_params=_params(),
        name="kv",
    )(h, g, wk, wvt)


def _q_kernel(h_ref, g_ref, wqt_ref, qt_ref):
    xn = _rms(h_ref[...], g_ref[...]).astype(BF16)
    qt = lax.dot_general(wqt_ref[...], xn, (((1,), (1,)), ((), ())),
                         preferred_element_type=F32)
    qt_ref[...] = (qt * Q_SCALE).astype(BF16)


def _qproj(h, g, wqt, batch, seq):
    n = seq // PROJ_ROWS
    row = pl.BlockSpec((PROJ_ROWS, D_MODEL), lambda b, i: (b * n + i, 0))
    return pl.pallas_call(
        _q_kernel,
        grid=(batch, n),
        in_specs=[row, _resident((1, D_MODEL)), _resident((D_MODEL, D_MODEL))],
        out_specs=pl.BlockSpec((None, D_MODEL, PROJ_ROWS), lambda b, i: (b, 0, i)),
        out_shape=jax.ShapeDtypeStruct((batch, D_MODEL, seq), BF16),
        compiler_params=_params(),
        name="qproj",
    )(h, g, wqt)


def _attn_kernel(qt_ref, k_ref, vt_ref, lam_ref, sg_ref, o_ref, qz_ref, acc_ref,
                 *, lambda_init):
    blk = ATTN_BLOCK
    hw = 2 * HEAD_DIM
    i = pl.program_id(2)

    zeros = jnp.zeros((HEAD_DIM, blk), BF16)
    for hd in range(ATTN_HEADS):
        qz_ref[hd, :HEAD_DIM, :blk] = qt_ref[hd * hw:hd * hw + HEAD_DIM, :]
        qz_ref[hd, :HEAD_DIM, blk:] = zeros
        qz_ref[hd, HEAD_DIM:, :blk] = zeros
        qz_ref[hd, HEAD_DIM:, blk:] = qt_ref[hd * hw + HEAD_DIM:(hd + 1) * hw, :]
    acc_ref[...] = jnp.zeros(acc_ref.shape, F32)

    def step(j, ms, masked):
        start = pl.multiple_of(j * blk, blk)
        k = k_ref[pl.ds(start, blk), :]
        ss = [_dot(k[:, hd * hw:(hd + 1) * hw], qz_ref[hd]) for hd in range(ATTN_HEADS)]
        out = []
        for hd in range(ATTN_HEADS):
            s = ss[hd]
            if masked:
                key = lax.broadcasted_iota(jnp.int32, (blk, 2 * blk), 0)
                qry = lax.broadcasted_iota(jnp.int32, (blk, 2 * blk), 1)
                qry = jnp.where(qry >= blk, qry - blk, qry)
                s = jnp.where(key <= qry, s, -jnp.inf)
            m_next = jnp.maximum(ms[hd], jnp.max(s, axis=0, keepdims=True))
            p = jnp.exp2(s - m_next).astype(BF16)
            alpha = jnp.exp2(ms[hd] - m_next)
            vt = vt_ref[hd * VT_ROWS:(hd + 1) * VT_ROWS, pl.ds(start, blk)]
            acc_ref[hd] = alpha * acc_ref[hd] + _dot(vt, p)
            out.append(m_next)
        return tuple(out)

    m0 = (jnp.full((1, 2 * blk), -jnp.inf, F32),) * ATTN_HEADS
    ms = lax.fori_loop(0, i, lambda j, ms: step(j, ms, masked=False), m0)
    step(i, ms, masked=True)

    lp = lam_ref[...]
    lam = (jnp.exp(jnp.sum(lp[0:1] * lp[1:2], axis=-1, keepdims=True))
           - jnp.exp(jnp.sum(lp[2:3] * lp[3:4], axis=-1, keepdims=True))
           + lambda_init)
    for hd in range(ATTN_HEADS):
        o1 = acc_ref[hd, :V_DIM, :blk] / acc_ref[hd, V_DIM:V_DIM + 1, :blk]
        o2 = acc_ref[hd, :V_DIM, blk:] / acc_ref[hd, V_DIM:V_DIM + 1, blk:]
        o = o1 - lam * o2
        ms_o = jnp.mean(o * o, axis=0, keepdims=True)
        o = o * lax.rsqrt(ms_o + RMS_EPS) * sg_ref[...] * (1.0 - lambda_init)
        o_ref[hd * V_DIM:(hd + 1) * V_DIM, :] = o.astype(BF16)


def _attn(qt, k, vt, lam_params, subln_g, lambda_init, batch, seq):
    blk = ATTN_BLOCK
    nq = seq // blk
    nh = ATTN_HEADS
    qo = pl.BlockSpec((None, nh * V_DIM, blk), lambda b, h, i: (b, h, i))
    return pl.pallas_call(
        functools.partial(_attn_kernel, lambda_init=lambda_init),
        grid=(batch, N_HEADS // nh, nq),
        in_specs=[qo,
                  pl.BlockSpec((seq, nh * 2 * HEAD_DIM), lambda b, h, i: (b, h)),
                  pl.BlockSpec((None, nh * VT_ROWS, seq), lambda b, h, i: (b, h, 0)),
                  _resident((4, HEAD_DIM)), _resident((V_DIM, 1))],
        out_specs=qo,
        out_shape=jax.ShapeDtypeStruct((batch, N_HEADS * V_DIM, seq), BF16),
        scratch_shapes=[pltpu.VMEM((nh, 2 * HEAD_DIM, 2 * blk), BF16),
                        pltpu.VMEM((nh, VT_ROWS, 2 * blk), F32)],
        compiler_params=_params(),
        name="attn",
    )(qt, k, vt, lam_params, subln_g)


def _o_kernel(h_ref, ot_ref, wo_ref, out_ref):
    o = lax.dot_general(ot_ref[...], wo_ref[...], (((0,), (0,)), ((), ())),
                        preferred_element_type=F32)
    out_ref[...] = h_ref[...] + o


def _oproj(h, ot, wo, batch, seq):
    n = seq // PROJ_ROWS
    row = pl.BlockSpec((PROJ_ROWS, D_MODEL), lambda b, i: (b * n + i, 0))
    return pl.pallas_call(
        _o_kernel,
        grid=(batch, n),
        in_specs=[row,
                  pl.BlockSpec((None, N_HEADS * V_DIM, PROJ_ROWS), lambda b, i: (b, 0, i)),
                  _resident((N_HEADS * V_DIM, D_MODEL))],
        out_specs=row,
        out_shape=jax.ShapeDtypeStruct((batch * seq, D_MODEL), F32),
        compiler_params=_params(),
        name="oproj",
    )(h, ot, wo)


def kernel(x, norm_g, ffn_w_gate_up, ffn_w_down, a_w_in, a_b_in, a_ln_g, a_ln_b, a_w_s, a_b_s, a_w_out, a_b_out, kv_norm_g, w_kv, b_w_q, b_lambda, b_subln_g, b_w_o, final_norm_g):
    batch, seq, d = x.shape
    h = x.reshape(batch * seq, d)
    fg = final_norm_g.reshape(1, d)
    k = vt = None
    for layer in range(DEPTH):
        h = _ffn(h, norm_g[layer, 0].reshape(1, d), ffn_w_gate_up[layer, 0].astype(BF16),
                 ffn_w_down[layer, 0].astype(BF16), fg, final_norm=False)
        g_mix = norm_g[layer, 1].reshape(1, d)
        if layer < N_A_LAYERS:
            a = layer
            h = _gmlp(h, g_mix, a_w_in[a].astype(BF16), a_b_in[a].reshape(1, -1),
                      a_ln_g[a].reshape(1, -1), a_ln_b[a].reshape(1, -1), a_w_s[a],
                      a_b_s[a].T, a_w_out[a].astype(BF16), a_b_out[a].reshape(1, -1))
        else:
            j = layer - N_A_LAYERS
            lambda_init = 0.8 - 0.6 * math.exp(-0.3 * layer)
            qt = _qproj(h, g_mix, b_w_q[j].T.astype(BF16), batch, seq)
            ot = _attn(qt, k, vt, b_lambda[j], b_subln_g[j].reshape(-1, 1), lambda_init,
                       batch, seq)
            h = _oproj(h, ot, b_w_o[j].astype(BF16), batch, seq)
        h = _ffn(h, norm_g[layer, 2].reshape(1, d), ffn_w_gate_up[layer, 1].astype(BF16),
                 ffn_w_down[layer, 1].astype(BF16), fg, final_norm=(layer == DEPTH - 1))
        if layer == N_A_LAYERS - 1:
            k, vt = _kv(h, kv_norm_g.reshape(1, d), w_kv[:, :K_WIDTH].astype(BF16),
                        w_kv[:, K_WIDTH:].T.astype(BF16), batch, seq)
    return h.reshape(batch, seq, d)
```

```python
import functools
import math

import jax
import jax.numpy as jnp
from jax import lax
from jax.experimental import pallas as pl
from jax.experimental.pallas import tpu as pltpu

D_MODEL = 1024
DEPTH = 4
N_A_LAYERS = DEPTH // 2
D_FF = 2816
D_GMLP = 6 * D_MODEL
D_GATE = D_GMLP // 2
SGU_GROUPS = 8
SGU_GROUP_DIM = D_GATE // SGU_GROUPS
CHUNK = 128
N_HEADS = 8
HEAD_DIM = 64
V_DIM = 128
K_WIDTH = N_HEADS * 2 * HEAD_DIM
RMS_EPS = 1e-6
LN_EPS = 1e-5

BF16 = jnp.bfloat16
F32 = jnp.float32

VMEM_LIMIT_BYTES = 56 * 1024 * 1024

FFN_ROWS = 512
FFN_COLS = 256
GMLP_ROWS = 512
GMLP_COLS = 768
PROJ_ROWS = 1024
ATTN_BLOCK = 512
ATTN_HEADS = 2
ONES_ROWS = 16
VT_ROWS = V_DIM + ONES_ROWS
Q_SCALE = HEAD_DIM ** -0.5 * math.log2(math.e)


def _resident(shape):
    return pl.BlockSpec(shape, lambda *_: (0,) * len(shape),
                        pipeline_mode=pl.Buffered(1))


def _stacked(shape, index):
    lead = (None,) * len(index)
    return pl.BlockSpec(lead + shape, lambda *_: tuple(index) + (0,) * len(shape),
                        pipeline_mode=pl.Buffered(1))


def _params():
    return pltpu.CompilerParams(vmem_limit_bytes=VMEM_LIMIT_BYTES)


def _rms(x, g):
    return x * lax.rsqrt(jnp.mean(x * x, axis=-1, keepdims=True) + RMS_EPS) * g


def _gelu(z):
    return 0.5 * z * (1.0 + lax.erf(z * (2.0 ** -0.5)))


def _dot(a, b):
    return jnp.dot(a, b, preferred_element_type=F32)


def _ffn_kernel(h_ref, g_ref, wgu_ref, wd_ref, fg_ref, o_ref, act_ref, *, final_norm):
    h = h_ref[...]
    xn = _rms(h, g_ref[...]).astype(BF16)
    for c in range(D_FF // FFN_COLS):
        lo = c * FFN_COLS
        gate = _dot(xn, wgu_ref[:, lo:lo + FFN_COLS])
        up = _dot(xn, wgu_ref[:, D_FF + lo:D_FF + lo + FFN_COLS])
        act = gate * (1.0 / (1.0 + jnp.exp(-gate))) * up
        act_ref[:, lo:lo + FFN_COLS] = act.astype(BF16)
    out = h + 0.5 * _dot(act_ref[...], wd_ref[...])
    if final_norm:
        out = _rms(out, fg_ref[...])
    o_ref[...] = out


def _ffn(h, g, wgu, wd, fg, index, final_norm):
    t = h.shape[0]
    row = pl.BlockSpec((FFN_ROWS, D_MODEL), lambda i: (i, 0))
    return pl.pallas_call(
        functools.partial(_ffn_kernel, final_norm=final_norm),
        grid=(t // FFN_ROWS,),
        in_specs=[row, _resident((1, D_MODEL)), _stacked((D_MODEL, 2 * D_FF), index),
                  _stacked((D_FF, D_MODEL), index), _resident((1, D_MODEL))],
        out_specs=row,
        out_shape=jax.ShapeDtypeStruct((t, D_MODEL), F32),
        scratch_shapes=[pltpu.VMEM((FFN_ROWS, D_FF), BF16)],
        compiler_params=_params(),
        name="ffn",
    )(h, g, wgu, wd, fg)


def _gmlp_kernel(h_ref, g_ref, win_ref, bin_ref, lng_ref, lnb_ref, ws_ref, bst_ref,
                 wout_ref, bout_ref, o_ref, v_ref, y_ref):
    h = h_ref[...]
    xn = _rms(h, g_ref[...]).astype(BF16)
    gd = SGU_GROUP_DIM
    cw = GMLP_COLS

    s1 = jnp.zeros((GMLP_ROWS, 1), F32)
    s2 = jnp.zeros((GMLP_ROWS, 1), F32)
    for c in range(D_GATE // cw):
        lo = D_GATE + c * cw
        v = _gelu(_dot(xn, win_ref[:, lo:lo + cw]) + bin_ref[:, lo:lo + cw])
        v_ref[:, c * cw:(c + 1) * cw] = v
        s1 = s1 + jnp.sum(v, axis=-1, keepdims=True)
        s2 = s2 + jnp.sum(v * v, axis=-1, keepdims=True)
    mu = s1 * (1.0 / D_GATE)
    var = s2 * (1.0 / D_GATE) - mu * mu
    rstd = lax.rsqrt(var + LN_EPS)

    rows = lax.broadcasted_iota(jnp.int32, (CHUNK, CHUNK), 0)
    cols = lax.broadcasted_iota(jnp.int32, (CHUNK, CHUNK), 1)
    causal = rows >= cols

    for c in range(D_GATE // cw):
        lo = c * cw
        u = _gelu(_dot(xn, win_ref[:, lo:lo + cw]) + bin_ref[:, lo:lo + cw])
        for g in range(c * (cw // gd), (c + 1) * (cw // gd)):
            glo = g * gd
            vn = ((v_ref[:, glo:glo + gd] - mu) * rstd * lng_ref[:, glo:glo + gd]
                  + lnb_ref[:, glo:glo + gd]).astype(BF16)
            ws = jnp.where(causal, ws_ref[g], 0.0).astype(BF16)
            bias = bst_ref[:, g:g + 1]
            s = jnp.concatenate(
                [_dot(ws, vn[r * CHUNK:(r + 1) * CHUNK]) + bias
                 for r in range(GMLP_ROWS // CHUNK)], axis=0)
            y_ref[:, glo:glo + gd] = (u[:, glo - lo:glo - lo + gd] * s).astype(BF16)

    o_ref[...] = h + _dot(y_ref[...], wout_ref[...]) + bout_ref[...]


def _gmlp(h, g, win, b_in, ln_g, ln_b, ws, bst, wout, b_out, index):
    t = h.shape[0]
    row = pl.BlockSpec((GMLP_ROWS, D_MODEL), lambda i: (i, 0))
    return pl.pallas_call(
        _gmlp_kernel,
        grid=(t // GMLP_ROWS,),
        in_specs=[row, _resident((1, D_MODEL)), _stacked((D_MODEL, D_GMLP), index),
                  _resident((1, D_GMLP)), _resident((1, D_GATE)), _resident((1, D_GATE)),
                  _resident((SGU_GROUPS, CHUNK, CHUNK)), _resident((CHUNK, SGU_GROUPS)),
                  _stacked((D_GATE, D_MODEL), index), _resident((1, D_MODEL))],
        out_specs=row,
        out_shape=jax.ShapeDtypeStruct((t, D_MODEL), F32),
        scratch_shapes=[pltpu.VMEM((GMLP_ROWS, D_GATE), F32),
                        pltpu.VMEM((GMLP_ROWS, D_GATE), BF16)],
        compiler_params=_params(),
        name="gmlp",
    )(h, g, win, b_in, ln_g, ln_b, ws, bst, wout, b_out)


def _kv_kernel(h_ref, g_ref, wk_ref, wvt_ref, k_ref, vt_ref):
    xn = _rms(h_ref[...], g_ref[...]).astype(BF16)
    k_ref[...] = _dot(xn, wk_ref[...]).astype(BF16)
    vt = lax.dot_general(wvt_ref[...], xn, (((1,), (1,)), ((), ())),
                         preferred_element_type=F32).astype(BF16)
    ones = jnp.ones((ONES_ROWS, PROJ_ROWS), BF16)
    for hd in range(N_HEADS):
        vt_ref[hd * VT_ROWS:hd * VT_ROWS + V_DIM, :] = vt[hd * V_DIM:(hd + 1) * V_DIM]
        vt_ref[hd * VT_ROWS + V_DIM:(hd + 1) * VT_ROWS, :] = ones


def _kv(h, g, wk, wvt, batch, seq):
    n = seq // PROJ_ROWS
    row = pl.BlockSpec((PROJ_ROWS, D_MODEL), lambda b, i: (b * n + i, 0))
    return pl.pallas_call(
        _kv_kernel,
        grid=(batch, n),
        in_specs=[row, _resident((1, D_MODEL)), _resident((D_MODEL, K_WIDTH)),
                  _resident((N_HEADS * V_DIM, D_MODEL))],
        out_specs=[row, pl.BlockSpec((None, N_HEADS * VT_ROWS, PROJ_ROWS),
                                     lambda b, i: (b, 0, i))],
        out_shape=[jax.ShapeDtypeStruct((batch * seq, K_WIDTH), BF16),
                   jax.ShapeDtypeStruct((batch, N_HEADS * VT_ROWS, seq), BF16)],
        compiler_params=_params(),
        name="kv",
    )(h, g, wk, wvt)


def _q_kernel(h_ref, g_ref, wqt_ref, qt_ref):
    xn = _rms(h_ref[...], g_ref[...]).astype(BF16)
    qt = lax.dot_general(wqt_ref[...], xn, (((1,), (1,)), ((), ())),
                         preferred_element_type=F32)
    qt_ref[...] = (qt * Q_SCALE).astype(BF16)


def _qproj(h, g, wqt, batch, seq):
    n = seq // PROJ_ROWS
    row = pl.BlockSpec((PROJ_ROWS, D_MODEL), lambda b, i: (b * n + i, 0))
    return pl.pallas_call(
        _q_kernel,
        grid=(batch, n),
        in_specs=[row, _resident((1, D_MODEL)), _resident((D_MODEL, D_MODEL))],
        out_specs=pl.BlockSpec((None, D_MODEL, PROJ_ROWS), lambda b, i: (b, 0, i)),
        out_shape=jax.ShapeDtypeStruct((batch, D_MODEL, seq), BF16),
        compiler_params=_params(),
        name="qproj",
    )(h, g, wqt)


def _attn_kernel(qt_ref, k_ref, vt_ref, lam_ref, sg_ref, o_ref, qz_ref, acc_ref,
                 sa_ref, sb_ref, *, lambda_init):
    blk = ATTN_BLOCK
    hw = 2 * HEAD_DIM
    i = pl.program_id(2)

    zeros = jnp.zeros((HEAD_DIM, blk), BF16)
    for hd in range(ATTN_HEADS):
        qz_ref[hd, :HEAD_DIM, :blk] = qt_ref[hd * hw:hd * hw + HEAD_DIM, :]
        qz_ref[hd, :HEAD_DIM, blk:] = zeros
        qz_ref[hd, HEAD_DIM:, :blk] = zeros
        qz_ref[hd, HEAD_DIM:, blk:] = qt_ref[hd * hw + HEAD_DIM:(hd + 1) * hw, :]
    acc_ref[...] = jnp.zeros(acc_ref.shape, F32)

    def scores(j, s_ref):
        start = pl.multiple_of(j * blk, blk)
        k = k_ref[pl.ds(start, blk), :]
        for hd in range(ATTN_HEADS):
            s_ref[hd] = _dot(k[:, hd * hw:(hd + 1) * hw], qz_ref[hd])

    def accumulate(j, s_ref, ms, masked):
        start = pl.multiple_of(j * blk, blk)
        out = []
        for hd in range(ATTN_HEADS):
            s = s_ref[hd]
            if masked:
                key = lax.broadcasted_iota(jnp.int32, (blk, 2 * blk), 0)
                qry = lax.broadcasted_iota(jnp.int32, (blk, 2 * blk), 1)
                qry = jnp.where(qry >= blk, qry - blk, qry)
                s = jnp.where(key <= qry, s, -jnp.inf)
            m_next = jnp.maximum(ms[hd], jnp.max(s, axis=0, keepdims=True))
            p = jnp.exp2(s - m_next).astype(BF16)
            alpha = jnp.exp2(ms[hd] - m_next)
            vt = vt_ref[hd * VT_ROWS:(hd + 1) * VT_ROWS, pl.ds(start, blk)]
            acc_ref[hd] = alpha * acc_ref[hd] + _dot(vt, p)
            out.append(m_next)
        return tuple(out)

    def pair(t, ms):
        j = 2 * t
        scores(j + 1, sb_ref)
        ms = accumulate(j, sa_ref, ms, masked=False)
        scores(j + 2, sa_ref)
        return accumulate(j + 1, sb_ref, ms, masked=False)

    scores(0, sa_ref)
    m0 = (jnp.full((1, 2 * blk), -jnp.inf, F32),) * ATTN_HEADS
    ms = lax.fori_loop(0, i // 2, pair, m0)

    @pl.when(i % 2 == 0)
    def _():
        accumulate(i, sa_ref, ms, masked=True)

    @pl.when(i % 2 == 1)
    def _():
        scores(i, sb_ref)
        ms_next = accumulate(i - 1, sa_ref, ms, masked=False)
        accumulate(i, sb_ref, ms_next, masked=True)

    lp = lam_ref[...]
    lam = (jnp.exp(jnp.sum(lp[0:1] * lp[1:2], axis=-1, keepdims=True))
           - jnp.exp(jnp.sum(lp[2:3] * lp[3:4], axis=-1, keepdims=True))
           + lambda_init)
    for hd in range(ATTN_HEADS):
        o1 = acc_ref[hd, :V_DIM, :blk] / acc_ref[hd, V_DIM:V_DIM + 1, :blk]
        o2 = acc_ref[hd, :V_DIM, blk:] / acc_ref[hd, V_DIM:V_DIM + 1, blk:]
        o = o1 - lam * o2
        ms_o = jnp.mean(o * o, axis=0, keepdims=True)
        o = o * lax.rsqrt(ms_o + RMS_EPS) * sg_ref[...] * (1.0 - lambda_init)
        o_ref[hd * V_DIM:(hd + 1) * V_DIM, :] = o.astype(BF16)


def _attn(qt, k, vt, lam_params, subln_g, lambda_init, batch, seq):
    blk = ATTN_BLOCK
    nq = seq // blk
    nh = ATTN_HEADS
    qo = pl.BlockSpec((None, nh * V_DIM, blk), lambda b, h, i: (b, h, i))
    return pl.pallas_call(
        functools.partial(_attn_kernel, lambda_init=lambda_init),
        grid=(batch, N_HEADS // nh, nq),
        in_specs=[qo,
                  pl.BlockSpec((seq, nh * 2 * HEAD_DIM), lambda b, h, i: (b, h)),
                  pl.BlockSpec((None, nh * VT_ROWS, seq), lambda b, h, i: (b, h, 0)),
                  _resident((4, HEAD_DIM)), _resident((V_DIM, 1))],
        out_specs=qo,
        out_shape=jax.ShapeDtypeStruct((batch, N_HEADS * V_DIM, seq), BF16),
        scratch_shapes=[pltpu.VMEM((nh, 2 * HEAD_DIM, 2 * blk), BF16),
                        pltpu.VMEM((nh, VT_ROWS, 2 * blk), F32),
                        pltpu.VMEM((nh, blk, 2 * blk), F32),
                        pltpu.VMEM((nh, blk, 2 * blk), F32)],
        compiler_params=_params(),
        name="attn",
    )(qt, k, vt, lam_params, subln_g)


def _o_kernel(h_ref, ot_ref, wo_ref, out_ref):
    o = lax.dot_general(ot_ref[...], wo_ref[...], (((0,), (0,)), ((), ())),
                        preferred_element_type=F32)
    out_ref[...] = h_ref[...] + o


def _oproj(h, ot, wo, batch, seq):
    n = seq // PROJ_ROWS
    row = pl.BlockSpec((PROJ_ROWS, D_MODEL), lambda b, i: (b * n + i, 0))
    return pl.pallas_call(
        _o_kernel,
        grid=(batch, n),
        in_specs=[row,
                  pl.BlockSpec((None, N_HEADS * V_DIM, PROJ_ROWS), lambda b, i: (b, 0, i)),
                  _resident((N_HEADS * V_DIM, D_MODEL))],
        out_specs=row,
        out_shape=jax.ShapeDtypeStruct((batch * seq, D_MODEL), F32),
        compiler_params=_params(),
        name="oproj",
    )(h, ot, wo)


def kernel(x, norm_g, ffn_w_gate_up, ffn_w_down, a_w_in, a_b_in, a_ln_g, a_ln_b, a_w_s, a_b_s, a_w_out, a_b_out, kv_norm_g, w_kv, b_w_q, b_lambda, b_subln_g, b_w_o, final_norm_g):
    batch, seq, d = x.shape
    h = x.reshape(batch * seq, d)
    fg = final_norm_g.reshape(1, d)
    wgu = ffn_w_gate_up.astype(BF16)
    wd = ffn_w_down.astype(BF16)
    win = a_w_in.astype(BF16)
    wout = a_w_out.astype(BF16)
    k = vt = None
    for layer in range(DEPTH):
        h = _ffn(h, norm_g[layer, 0].reshape(1, d), wgu, wd, fg, (layer, 0), final_norm=False)
        g_mix = norm_g[layer, 1].reshape(1, d)
        if layer < N_A_LAYERS:
            a = layer
            h = _gmlp(h, g_mix, win, a_b_in[a].reshape(1, -1),
                      a_ln_g[a].reshape(1, -1), a_ln_b[a].reshape(1, -1), a_w_s[a],
                      a_b_s[a].T, wout, a_b_out[a].reshape(1, -1), (a,))
        else:
            j = layer - N_A_LAYERS
            lambda_init = 0.8 - 0.6 * math.exp(-0.3 * layer)
            qt = _qproj(h, g_mix, b_w_q[j].T.astype(BF16), batch, seq)
            ot = _attn(qt, k, vt, b_lambda[j], b_subln_g[j].reshape(-1, 1), lambda_init,
                       batch, seq)
            h = _oproj(h, ot, b_w_o[j].astype(BF16), batch, seq)
        h = _ffn(h, norm_g[layer, 2].reshape(1, d), wgu, wd, fg, (layer, 1),
                 final_norm=(layer == DEPTH - 1))
        if layer == N_A_LAYERS - 1:
            k, vt = _kv(h, kv_norm_g.reshape(1, d), w_kv[:, :K_WIDTH].astype(BF16),
                        w_kv[:, K_WIDTH:].T.astype(BF16), batch, seq)
    return h.reshape(batch, seq, d)
```

```python
import functools
import math

import jax
import jax.numpy as jnp
from jax import lax
from jax.experimental import pallas as pl
from jax.experimental.pallas import tpu as pltpu

D_MODEL = 1024
DEPTH = 4
N_A_LAYERS = DEPTH // 2
D_FF = 2816
D_GMLP = 6 * D_MODEL
D_GATE = D_GMLP // 2
SGU_GROUPS = 8
SGU_GROUP_DIM = D_GATE // SGU_GROUPS
CHUNK = 128
N_HEADS = 8
HEAD_DIM = 64
V_DIM = 128
K_WIDTH = N_HEADS * 2 * HEAD_DIM
RMS_EPS = 1e-6
LN_EPS = 1e-5

BF16 = jnp.bfloat16
F32 = jnp.float32

VMEM_LIMIT_BYTES = 56 * 1024 * 1024

FFN_ROWS = 1024
FFN_COLS = 256
GMLP_ROWS = 512
GMLP_COLS = 768
PROJ_ROWS = 1024
ATTN_BLOCK = 512
ATTN_HEADS = 2
ONES_ROWS = 16
VT_ROWS = V_DIM + ONES_ROWS
Q_SCALE = HEAD_DIM ** -0.5 * math.log2(math.e)


def _resident(shape):
    return pl.BlockSpec(shape, lambda *_: (0,) * len(shape),
                        pipeline_mode=pl.Buffered(1))


def _stacked(shape, index):
    lead = (None,) * len(index)
    return pl.BlockSpec(lead + shape, lambda *_: tuple(index) + (0,) * len(shape),
                        pipeline_mode=pl.Buffered(1))


def _params():
    return pltpu.CompilerParams(vmem_limit_bytes=VMEM_LIMIT_BYTES)


def _rms(x, g):
    return x * lax.rsqrt(jnp.mean(x * x, axis=-1, keepdims=True) + RMS_EPS) * g


def _gelu(z):
    return 0.5 * z * (1.0 + lax.erf(z * (2.0 ** -0.5)))


def _dot(a, b):
    return jnp.dot(a, b, preferred_element_type=F32)


def _ffn_kernel(h_ref, g_ref, wgu_ref, wd_ref, fg_ref, o_ref, act_ref, *, final_norm):
    h = h_ref[...]
    xn = _rms(h, g_ref[...]).astype(BF16)
    for c in range(D_FF // FFN_COLS):
        lo = c * FFN_COLS
        gate = _dot(xn, wgu_ref[:, lo:lo + FFN_COLS])
        up = _dot(xn, wgu_ref[:, D_FF + lo:D_FF + lo + FFN_COLS])
        act = gate * (1.0 / (1.0 + jnp.exp(-gate))) * up
        act_ref[:, lo:lo + FFN_COLS] = act.astype(BF16)
    out = h + 0.5 * _dot(act_ref[...], wd_ref[...])
    if final_norm:
        out = _rms(out, fg_ref[...])
    o_ref[...] = out


def _ffn(h, g, wgu, wd, fg, index, final_norm):
    t = h.shape[0]
    row = pl.BlockSpec((FFN_ROWS, D_MODEL), lambda i: (i, 0))
    return pl.pallas_call(
        functools.partial(_ffn_kernel, final_norm=final_norm),
        grid=(t // FFN_ROWS,),
        in_specs=[row, _resident((1, D_MODEL)), _stacked((D_MODEL, 2 * D_FF), index),
                  _stacked((D_FF, D_MODEL), index), _resident((1, D_MODEL))],
        out_specs=row,
        out_shape=jax.ShapeDtypeStruct((t, D_MODEL), F32),
        scratch_shapes=[pltpu.VMEM((FFN_ROWS, D_FF), BF16)],
        compiler_params=_params(),
        name="ffn",
    )(h, g, wgu, wd, fg)


def _gmlp_kernel(h_ref, g_ref, win_ref, bin_ref, lng_ref, lnb_ref, ws_ref, bst_ref,
                 wout_ref, bout_ref, o_ref, v_ref, y_ref):
    h = h_ref[...]
    xn = _rms(h, g_ref[...]).astype(BF16)
    gd = SGU_GROUP_DIM
    cw = GMLP_COLS

    s1 = jnp.zeros((GMLP_ROWS, 1), F32)
    s2 = jnp.zeros((GMLP_ROWS, 1), F32)
    for c in range(D_GATE // cw):
        lo = D_GATE + c * cw
        v = _gelu(_dot(xn, win_ref[:, lo:lo + cw]) + bin_ref[:, lo:lo + cw])
        v_ref[:, c * cw:(c + 1) * cw] = v
        s1 = s1 + jnp.sum(v, axis=-1, keepdims=True)
        s2 = s2 + jnp.sum(v * v, axis=-1, keepdims=True)
    mu = s1 * (1.0 / D_GATE)
    var = s2 * (1.0 / D_GATE) - mu * mu
    rstd = lax.rsqrt(var + LN_EPS)

    rows = lax.broadcasted_iota(jnp.int32, (CHUNK, CHUNK), 0)
    cols = lax.broadcasted_iota(jnp.int32, (CHUNK, CHUNK), 1)
    causal = rows >= cols

    for c in range(D_GATE // cw):
        lo = c * cw
        u = _gelu(_dot(xn, win_ref[:, lo:lo + cw]) + bin_ref[:, lo:lo + cw])
        for g in range(c * (cw // gd), (c + 1) * (cw // gd)):
            glo = g * gd
            vn = ((v_ref[:, glo:glo + gd] - mu) * rstd * lng_ref[:, glo:glo + gd]
                  + lnb_ref[:, glo:glo + gd]).astype(BF16)
            ws = jnp.where(causal, ws_ref[g], 0.0).astype(BF16)
            bias = bst_ref[:, g:g + 1]
            s = jnp.concatenate(
                [_dot(ws, vn[r * CHUNK:(r + 1) * CHUNK]) + bias
                 for r in range(GMLP_ROWS // CHUNK)], axis=0)
            y_ref[:, glo:glo + gd] = (u[:, glo - lo:glo - lo + gd] * s).astype(BF16)

    o_ref[...] = h + _dot(y_ref[...], wout_ref[...]) + bout_ref[...]


def _gmlp(h, g, win, b_in, ln_g, ln_b, ws, bst, wout, b_out, index):
    t = h.shape[0]
    row = pl.BlockSpec((GMLP_ROWS, D_MODEL), lambda i: (i, 0))
    return pl.pallas_call(
        _gmlp_kernel,
        grid=(t // GMLP_ROWS,),
        in_specs=[row, _resident((1, D_MODEL)), _stacked((D_MODEL, D_GMLP), index),
                  _resident((1, D_GMLP)), _resident((1, D_GATE)), _resident((1, D_GATE)),
                  _resident((SGU_GROUPS, CHUNK, CHUNK)), _resident((CHUNK, SGU_GROUPS)),
                  _stacked((D_GATE, D_MODEL), index), _resident((1, D_MODEL))],
        out_specs=row,
        out_shape=jax.ShapeDtypeStruct((t, D_MODEL), F32),
        scratch_shapes=[pltpu.VMEM((GMLP_ROWS, D_GATE), F32),
                        pltpu.VMEM((GMLP_ROWS, D_GATE), BF16)],
        compiler_params=_params(),
        name="gmlp",
    )(h, g, win, b_in, ln_g, ln_b, ws, bst, wout, b_out)


def _kv_kernel(h_ref, g_ref, wk_ref, wvt_ref, k_ref, vt_ref):
    xn = _rms(h_ref[...], g_ref[...]).astype(BF16)
    k_ref[...] = _dot(xn, wk_ref[...]).astype(BF16)
    vt = lax.dot_general(wvt_ref[...], xn, (((1,), (1,)), ((), ())),
                         preferred_element_type=F32).astype(BF16)
    ones = jnp.ones((ONES_ROWS, PROJ_ROWS), BF16)
    for hd in range(N_HEADS):
        vt_ref[hd * VT_ROWS:hd * VT_ROWS + V_DIM, :] = vt[hd * V_DIM:(hd + 1) * V_DIM]
        vt_ref[hd * VT_ROWS + V_DIM:(hd + 1) * VT_ROWS, :] = ones


def _kv(h, g, wk, wvt, batch, seq):
    n = seq // PROJ_ROWS
    row = pl.BlockSpec((PROJ_ROWS, D_MODEL), lambda b, i: (b * n + i, 0))
    return pl.pallas_call(
        _kv_kernel,
        grid=(batch, n),
        in_specs=[row, _resident((1, D_MODEL)), _resident((D_MODEL, K_WIDTH)),
                  _resident((N_HEADS * V_DIM, D_MODEL))],
        out_specs=[row, pl.BlockSpec((None, N_HEADS * VT_ROWS, PROJ_ROWS),
                                     lambda b, i: (b, 0, i))],
        out_shape=[jax.ShapeDtypeStruct((batch * seq, K_WIDTH), BF16),
                   jax.ShapeDtypeStruct((batch, N_HEADS * VT_ROWS, seq), BF16)],
        compiler_params=_params(),
        name="kv",
    )(h, g, wk, wvt)


def _q_kernel(h_ref, g_ref, wqt_ref, qt_ref):
    xn = _rms(h_ref[...], g_ref[...]).astype(BF16)
    qt = lax.dot_general(wqt_ref[...], xn, (((1,), (1,)), ((), ())),
                         preferred_element_type=F32)
    qt_ref[...] = (qt * Q_SCALE).astype(BF16)


def _qproj(h, g, wqt, batch, seq):
    n = seq // PROJ_ROWS
    row = pl.BlockSpec((PROJ_ROWS, D_MODEL), lambda b, i: (b * n + i, 0))
    return pl.pallas_call(
        _q_kernel,
        grid=(batch, n),
        in_specs=[row, _resident((1, D_MODEL)), _resident((D_MODEL, D_MODEL))],
        out_specs=pl.BlockSpec((None, D_MODEL, PROJ_ROWS), lambda b, i: (b, 0, i)),
        out_shape=jax.ShapeDtypeStruct((batch, D_MODEL, seq), BF16),
        compiler_params=_params(),
        name="qproj",
    )(h, g, wqt)


def _attn_kernel(qt_ref, k_ref, vt_ref, lam_ref, sg_ref, o_ref, qz_ref, acc_ref,
                 sa_ref, sb_ref, mxa_ref, mxb_ref, *, lambda_init, n_tiles):
    blk = ATTN_BLOCK
    hw = 2 * HEAD_DIM
    heads = range(ATTN_HEADS)
    bufs = ((sa_ref, mxa_ref), (sb_ref, mxb_ref))

    lp = lam_ref[...]
    lam = (jnp.exp(jnp.sum(lp[0:1] * lp[1:2], axis=-1, keepdims=True))
           - jnp.exp(jnp.sum(lp[2:3] * lp[3:4], axis=-1, keepdims=True))
           + lambda_init)

    def block_start(j):
        return j * blk if isinstance(j, int) else pl.multiple_of(j * blk, blk)

    def load_queries(i):
        zeros = jnp.zeros((HEAD_DIM, blk), BF16)
        cols = slice(i * blk, (i + 1) * blk)
        for hd in heads:
            qz_ref[i % 2, hd, :HEAD_DIM, :blk] = qt_ref[hd * hw:hd * hw + HEAD_DIM, cols]
            qz_ref[i % 2, hd, :HEAD_DIM, blk:] = zeros
            qz_ref[i % 2, hd, HEAD_DIM:, :blk] = zeros
            qz_ref[i % 2, hd, HEAD_DIM:, blk:] = qt_ref[hd * hw + HEAD_DIM:(hd + 1) * hw, cols]
        acc_ref[i % 2] = jnp.zeros(acc_ref.shape[1:], F32)

    def scores(i, j, buf):
        s_ref, mx_ref = buf
        k = k_ref[pl.ds(block_start(j), blk), :]
        for hd in heads:
            s = _dot(k[:, hd * hw:(hd + 1) * hw], qz_ref[i % 2, hd])
            s_ref[hd] = s
            mx_ref[hd] = jnp.max(s, axis=0, keepdims=True)

    def accumulate(i, j, buf, ms, masked):
        s_ref, mx_ref = buf
        out = []
        for hd in heads:
            s = s_ref[hd]
            if masked:
                key = lax.broadcasted_iota(jnp.int32, (blk, 2 * blk), 0)
                qry = lax.broadcasted_iota(jnp.int32, (blk, 2 * blk), 1)
                qry = jnp.where(qry >= blk, qry - blk, qry)
                s = jnp.where(key <= qry, s, -jnp.inf)
                mx = jnp.max(s, axis=0, keepdims=True)
            else:
                mx = mx_ref[hd]
            m_next = jnp.maximum(ms[hd], mx)
            p = jnp.exp2(s - m_next).astype(BF16)
            alpha = jnp.exp2(ms[hd] - m_next)
            vt = vt_ref[hd * VT_ROWS:(hd + 1) * VT_ROWS, pl.ds(block_start(j), blk)]
            acc_ref[i % 2, hd] = alpha * acc_ref[i % 2, hd] + _dot(vt, p)
            out.append(m_next)
        return tuple(out)

    def finish(i):
        for hd in heads:
            acc = acc_ref.at[i % 2, hd]
            o1 = acc[:V_DIM, :blk] / acc[V_DIM:V_DIM + 1, :blk]
            o2 = acc[:V_DIM, blk:] / acc[V_DIM:V_DIM + 1, blk:]
            o = o1 - lam * o2
            ms_o = jnp.mean(o * o, axis=0, keepdims=True)
            o = o * lax.rsqrt(ms_o + RMS_EPS) * sg_ref[...] * (1.0 - lambda_init)
            o_ref[hd * V_DIM:(hd + 1) * V_DIM, i * blk:(i + 1) * blk] = o.astype(BF16)

    def start_next(i, buf):
        if i + 1 < n_tiles:
            load_queries(i + 1)
            scores(i + 1, 0, buf)

    load_queries(0)
    scores(0, 0, bufs[0])
    first = 0
    for i in range(n_tiles):
        b0, b1 = bufs[first], bufs[1 - first]

        def pair(t, ms, i=i, b0=b0, b1=b1):
            j = 2 * t
            scores(i, j + 1, b1)
            ms = accumulate(i, j, b0, ms, masked=False)
            scores(i, j + 2, b0)
            return accumulate(i, j + 1, b1, ms, masked=False)

        ms = (jnp.full((1, 2 * blk), -jnp.inf, F32),) * ATTN_HEADS
        if i // 2 > 0:
            ms = lax.fori_loop(0, i // 2, pair, ms)
        if i % 2 == 0:
            start_next(i, b1)
            accumulate(i, i, b0, ms, masked=True)
            first = 1 - first
        else:
            scores(i, i, b1)
            ms = accumulate(i, i - 1, b0, ms, masked=False)
            start_next(i, b0)
            accumulate(i, i, b1, ms, masked=True)
        finish(i)


def _attn(qt, k, vt, lam_params, subln_g, lambda_init, batch, seq):
    blk = ATTN_BLOCK
    nh = ATTN_HEADS
    qo = pl.BlockSpec((None, nh * V_DIM, seq), lambda b, h: (b, h, 0))
    return pl.pallas_call(
        functools.partial(_attn_kernel, lambda_init=lambda_init, n_tiles=seq // blk),
        grid=(batch, N_HEADS // nh),
        in_specs=[qo,
                  pl.BlockSpec((seq, nh * 2 * HEAD_DIM), lambda b, h: (b, h)),
                  pl.BlockSpec((None, nh * VT_ROWS, seq), lambda b, h: (b, h, 0)),
                  _resident((4, HEAD_DIM)), _resident((V_DIM, 1))],
        out_specs=qo,
        out_shape=jax.ShapeDtypeStruct((batch, N_HEADS * V_DIM, seq), BF16),
        scratch_shapes=[pltpu.VMEM((2, nh, 2 * HEAD_DIM, 2 * blk), BF16),
                        pltpu.VMEM((2, nh, VT_ROWS, 2 * blk), F32),
                        pltpu.VMEM((nh, blk, 2 * blk), F32),
                        pltpu.VMEM((nh, blk, 2 * blk), F32),
                        pltpu.VMEM((nh, 1, 2 * blk), F32),
                        pltpu.VMEM((nh, 1, 2 * blk), F32)],
        compiler_params=_params(),
        name="attn",
    )(qt, k, vt, lam_params, subln_g)


def _o_kernel(h_ref, ot_ref, wo_ref, out_ref):
    o = lax.dot_general(ot_ref[...], wo_ref[...], (((0,), (0,)), ((), ())),
                        preferred_element_type=F32)
    out_ref[...] = h_ref[...] + o


def _oproj(h, ot, wo, batch, seq):
    n = seq // PROJ_ROWS
    row = pl.BlockSpec((PROJ_ROWS, D_MODEL), lambda b, i: (b * n + i, 0))
    return pl.pallas_call(
        _o_kernel,
        grid=(batch, n),
        in_specs=[row,
                  pl.BlockSpec((None, N_HEADS * V_DIM, PROJ_ROWS), lambda b, i: (b, 0, i)),
                  _resident((N_HEADS * V_DIM, D_MODEL))],
        out_specs=row,
        out_shape=jax.ShapeDtypeStruct((batch * seq, D_MODEL), F32),
        compiler_params=_params(),
        name="oproj",
    )(h, ot, wo)


def kernel(x, norm_g, ffn_w_gate_up, ffn_w_down, a_w_in, a_b_in, a_ln_g, a_ln_b, a_w_s, a_b_s, a_w_out, a_b_out, kv_norm_g, w_kv, b_w_q, b_lambda, b_subln_g, b_w_o, final_norm_g):
    batch, seq, d = x.shape
    h = x.reshape(batch * seq, d)
    fg = final_norm_g.reshape(1, d)
    wgu = ffn_w_gate_up.astype(BF16)
    wd = ffn_w_down.astype(BF16)
    win = a_w_in.astype(BF16)
    wout = a_w_out.astype(BF16)
    k = vt = None
    for layer in range(DEPTH):
        h = _ffn(h, norm_g[layer, 0].reshape(1, d), wgu, wd, fg, (layer, 0), final_norm=False)
        g_mix = norm_g[layer, 1].reshape(1, d)
        if layer < N_A_LAYERS:
            a = layer
            h = _gmlp(h, g_mix, win, a_b_in[a].reshape(1, -1),
                      a_ln_g[a].reshape(1, -1), a_ln_b[a].reshape(1, -1), a_w_s[a],
                      a_b_s[a].T, wout, a_b_out[a].reshape(1, -1), (a,))
        else:
            j = layer - N_A_LAYERS
            lambda_init = 0.8 - 0.6 * math.exp(-0.3 * layer)
            qt = _qproj(h, g_mix, b_w_q[j].T.astype(BF16), batch, seq)
            ot = _attn(qt, k, vt, b_lambda[j], b_subln_g[j].reshape(-1, 1), lambda_init,
                       batch, seq)
            h = _oproj(h, ot, b_w_o[j].astype(BF16), batch, seq)
        h = _ffn(h, norm_g[layer, 2].reshape(1, d), wgu, wd, fg, (layer, 1),
                 final_norm=(layer == DEPTH - 1))
        if layer == N_A_LAYERS - 1:
            k, vt = _kv(h, kv_norm_g.reshape(1, d), w_kv[:, :K_WIDTH].astype(BF16),
                        w_kv[:, K_WIDTH:].T.astype(BF16), batch, seq)
    return h.reshape(batch, seq, d)
```

```python
import functools
import math

import jax
import jax.numpy as jnp
from jax import lax
from jax.experimental import pallas as pl
from jax.experimental.pallas import tpu as pltpu

D_MODEL = 1024
DEPTH = 4
N_A_LAYERS = DEPTH // 2
D_FF = 2816
D_GMLP = 6 * D_MODEL
D_GATE = D_GMLP // 2
SGU_GROUPS = 8
SGU_GROUP_DIM = D_GATE // SGU_GROUPS
CHUNK = 128
N_HEADS = 8
HEAD_DIM = 64
V_DIM = 128
K_WIDTH = N_HEADS * 2 * HEAD_DIM
RMS_EPS = 1e-6
LN_EPS = 1e-5

BF16 = jnp.bfloat16
F32 = jnp.float32

VMEM_LIMIT_BYTES = 56 * 1024 * 1024

FFN_ROWS = 1024
FFN_COLS = 256
GMLP_ROWS = 512
GMLP_COLS = 768
PROJ_ROWS = 1024
ATTN_BLOCK = 512
ATTN_HEADS = 2
ONES_ROWS = 16
VT_ROWS = V_DIM + ONES_ROWS
Q_SCALE = HEAD_DIM ** -0.5 * math.log2(math.e)


def _resident(shape):
    return pl.BlockSpec(shape, lambda *_: (0,) * len(shape),
                        pipeline_mode=pl.Buffered(1))


def _stacked(shape, index):
    lead = (None,) * len(index)
    return pl.BlockSpec(lead + shape, lambda *_: tuple(index) + (0,) * len(shape),
                        pipeline_mode=pl.Buffered(1))


def _params():
    return pltpu.CompilerParams(vmem_limit_bytes=VMEM_LIMIT_BYTES)


def _rms(x, g):
    return x * lax.rsqrt(jnp.mean(x * x, axis=-1, keepdims=True) + RMS_EPS) * g


def _gelu(z):
    return 0.5 * z * (1.0 + lax.erf(z * (2.0 ** -0.5)))


def _dot(a, b):
    return jnp.dot(a, b, preferred_element_type=F32)


def _ffn_kernel(h_ref, g_ref, wgu_ref, wd_ref, fg_ref, o_ref, act_ref, *, final_norm):
    h = h_ref[...]
    xn = _rms(h, g_ref[...]).astype(BF16)
    for c in range(D_FF // FFN_COLS):
        lo = c * FFN_COLS
        gate = _dot(xn, wgu_ref[:, lo:lo + FFN_COLS])
        up = _dot(xn, wgu_ref[:, D_FF + lo:D_FF + lo + FFN_COLS])
        act = gate * (1.0 / (1.0 + jnp.exp(-gate))) * up
        act_ref[:, lo:lo + FFN_COLS] = act.astype(BF16)
    out = h + 0.5 * _dot(act_ref[...], wd_ref[...])
    if final_norm:
        out = _rms(out, fg_ref[...])
    o_ref[...] = out


def _ffn(h, g, wgu, wd, fg, index, final_norm):
    t = h.shape[0]
    row = pl.BlockSpec((FFN_ROWS, D_MODEL), lambda i: (i, 0))
    return pl.pallas_call(
        functools.partial(_ffn_kernel, final_norm=final_norm),
        grid=(t // FFN_ROWS,),
        in_specs=[row, _resident((1, D_MODEL)), _stacked((D_MODEL, 2 * D_FF), index),
                  _stacked((D_FF, D_MODEL), index), _resident((1, D_MODEL))],
        out_specs=row,
        out_shape=jax.ShapeDtypeStruct((t, D_MODEL), F32),
        scratch_shapes=[pltpu.VMEM((FFN_ROWS, D_FF), BF16)],
        compiler_params=_params(),
        name="ffn",
    )(h, g, wgu, wd, fg)


def _gmlp_kernel(h_ref, g_ref, win_ref, bin_ref, lng_ref, lnb_ref, ws_ref, bst_ref,
                 wout_ref, bout_ref, o_ref, v_ref, y_ref):
    h = h_ref[...]
    xn = _rms(h, g_ref[...]).astype(BF16)
    gd = SGU_GROUP_DIM
    cw = GMLP_COLS

    s1 = jnp.zeros((GMLP_ROWS, 1), F32)
    s2 = jnp.zeros((GMLP_ROWS, 1), F32)
    for c in range(D_GATE // cw):
        lo = D_GATE + c * cw
        v = _gelu(_dot(xn, win_ref[:, lo:lo + cw]) + bin_ref[:, lo:lo + cw])
        v_ref[:, c * cw:(c + 1) * cw] = v
        s1 = s1 + jnp.sum(v, axis=-1, keepdims=True)
        s2 = s2 + jnp.sum(v * v, axis=-1, keepdims=True)
    mu = s1 * (1.0 / D_GATE)
    var = s2 * (1.0 / D_GATE) - mu * mu
    rstd = lax.rsqrt(var + LN_EPS)

    rows = lax.broadcasted_iota(jnp.int32, (CHUNK, CHUNK), 0)
    cols = lax.broadcasted_iota(jnp.int32, (CHUNK, CHUNK), 1)
    causal = rows >= cols

    for c in range(D_GATE // cw):
        lo = c * cw
        u = _gelu(_dot(xn, win_ref[:, lo:lo + cw]) + bin_ref[:, lo:lo + cw])
        for g in range(c * (cw // gd), (c + 1) * (cw // gd)):
            glo = g * gd
            vn = ((v_ref[:, glo:glo + gd] - mu) * rstd * lng_ref[:, glo:glo + gd]
                  + lnb_ref[:, glo:glo + gd]).astype(BF16)
            ws = jnp.where(causal, ws_ref[g], 0.0).astype(BF16)
            bias = bst_ref[:, g:g + 1]
            s = jnp.concatenate(
                [_dot(ws, vn[r * CHUNK:(r + 1) * CHUNK]) + bias
                 for r in range(GMLP_ROWS // CHUNK)], axis=0)
            y_ref[:, glo:glo + gd] = (u[:, glo - lo:glo - lo + gd] * s).astype(BF16)

    o_ref[...] = h + _dot(y_ref[...], wout_ref[...]) + bout_ref[...]


def _gmlp(h, g, win, b_in, ln_g, ln_b, ws, bst, wout, b_out, index):
    t = h.shape[0]
    row = pl.BlockSpec((GMLP_ROWS, D_MODEL), lambda i: (i, 0))
    return pl.pallas_call(
        _gmlp_kernel,
        grid=(t // GMLP_ROWS,),
        in_specs=[row, _resident((1, D_MODEL)), _stacked((D_MODEL, D_GMLP), index),
                  _resident((1, D_GMLP)), _resident((1, D_GATE)), _resident((1, D_GATE)),
                  _resident((SGU_GROUPS, CHUNK, CHUNK)), _resident((CHUNK, SGU_GROUPS)),
                  _stacked((D_GATE, D_MODEL), index), _resident((1, D_MODEL))],
        out_specs=row,
        out_shape=jax.ShapeDtypeStruct((t, D_MODEL), F32),
        scratch_shapes=[pltpu.VMEM((GMLP_ROWS, D_GATE), F32),
                        pltpu.VMEM((GMLP_ROWS, D_GATE), BF16)],
        compiler_params=_params(),
        name="gmlp",
    )(h, g, win, b_in, ln_g, ln_b, ws, bst, wout, b_out)


def _kv_kernel(h_ref, g_ref, wk_ref, wvt_ref, k_ref, vt_ref):
    xn = _rms(h_ref[...], g_ref[...]).astype(BF16)
    k_ref[...] = _dot(xn, wk_ref[...]).astype(BF16)
    vt = lax.dot_general(wvt_ref[...], xn, (((1,), (1,)), ((), ())),
                         preferred_element_type=F32).astype(BF16)
    ones = jnp.ones((ONES_ROWS, PROJ_ROWS), BF16)
    for hd in range(N_HEADS):
        vt_ref[hd * VT_ROWS:hd * VT_ROWS + V_DIM, :] = vt[hd * V_DIM:(hd + 1) * V_DIM]
        vt_ref[hd * VT_ROWS + V_DIM:(hd + 1) * VT_ROWS, :] = ones


def _kv(h, g, wk, wvt, batch, seq):
    n = seq // PROJ_ROWS
    row = pl.BlockSpec((PROJ_ROWS, D_MODEL), lambda b, i: (b * n + i, 0))
    return pl.pallas_call(
        _kv_kernel,
        grid=(batch, n),
        in_specs=[row, _resident((1, D_MODEL)), _resident((D_MODEL, K_WIDTH)),
                  _resident((N_HEADS * V_DIM, D_MODEL))],
        out_specs=[row, pl.BlockSpec((None, N_HEADS * VT_ROWS, PROJ_ROWS),
                                     lambda b, i: (b, 0, i))],
        out_shape=[jax.ShapeDtypeStruct((batch * seq, K_WIDTH), BF16),
                   jax.ShapeDtypeStruct((batch, N_HEADS * VT_ROWS, seq), BF16)],
        compiler_params=_params(),
        name="kv",
    )(h, g, wk, wvt)


def _q_kernel(h_ref, g_ref, wqt_ref, qt_ref):
    xn = _rms(h_ref[...], g_ref[...]).astype(BF16)
    qt = lax.dot_general(wqt_ref[...], xn, (((1,), (1,)), ((), ())),
                         preferred_element_type=F32)
    qt_ref[...] = (qt * Q_SCALE).astype(BF16)


def _qproj(h, g, wqt, batch, seq):
    n = seq // PROJ_ROWS
    row = pl.BlockSpec((PROJ_ROWS, D_MODEL), lambda b, i: (b * n + i, 0))
    return pl.pallas_call(
        _q_kernel,
        grid=(batch, n),
        in_specs=[row, _resident((1, D_MODEL)), _resident((D_MODEL, D_MODEL))],
        out_specs=pl.BlockSpec((None, D_MODEL, PROJ_ROWS), lambda b, i: (b, 0, i)),
        out_shape=jax.ShapeDtypeStruct((batch, D_MODEL, seq), BF16),
        compiler_params=_params(),
        name="qproj",
    )(h, g, wqt)


def _attn_kernel(qt_ref, k_ref, vt_ref, lam_ref, sg_ref, o_ref, qz_ref, acc_ref,
                 sa_ref, sb_ref, mxa_ref, mxb_ref, *, lambda_init, n_tiles):
    blk = ATTN_BLOCK
    half = blk // 2
    hw = 2 * HEAD_DIM
    heads = range(ATTN_HEADS)
    bufs = ((sa_ref, mxa_ref), (sb_ref, mxb_ref))

    lp = lam_ref[...]
    lam = (jnp.exp(jnp.sum(lp[0:1] * lp[1:2], axis=-1, keepdims=True))
           - jnp.exp(jnp.sum(lp[2:3] * lp[3:4], axis=-1, keepdims=True))
           + lambda_init)

    def block_start(j):
        return j * blk if isinstance(j, int) else pl.multiple_of(j * blk, blk)

    def load_queries(i):
        zeros = jnp.zeros((HEAD_DIM, half), BF16)
        for hd in heads:
            for part in range(2):
                cols = slice(i * blk + part * half, i * blk + (part + 1) * half)
                lo = part * blk
                q1 = qt_ref[hd * hw:hd * hw + HEAD_DIM, cols]
                q2 = qt_ref[hd * hw + HEAD_DIM:(hd + 1) * hw, cols]
                qz_ref[i % 2, hd, :HEAD_DIM, lo:lo + half] = q1
                qz_ref[i % 2, hd, HEAD_DIM:, lo:lo + half] = zeros
                qz_ref[i % 2, hd, :HEAD_DIM, lo + half:lo + blk] = zeros
                qz_ref[i % 2, hd, HEAD_DIM:, lo + half:lo + blk] = q2

    def scores(i, j, buf):
        s_ref, mx_ref = buf
        k = k_ref[pl.ds(block_start(j), blk), :]
        for hd in heads:
            s = _dot(k[:, hd * hw:(hd + 1) * hw], qz_ref[i % 2, hd])
            s_ref[hd] = s
            mx_ref[hd] = jnp.max(s, axis=0, keepdims=True)

    def accumulate(i, j, buf, ms):
        s_ref, mx_ref = buf
        out = []
        for hd in heads:
            m_next = jnp.maximum(ms[hd], mx_ref[hd])
            p = jnp.exp2(s_ref[hd] - m_next).astype(BF16)
            alpha = jnp.exp2(ms[hd] - m_next)
            vt = vt_ref[hd * VT_ROWS:(hd + 1) * VT_ROWS, pl.ds(block_start(j), blk)]
            acc_ref[i % 2, hd] = alpha * acc_ref[i % 2, hd] + _dot(vt, p)
            out.append(m_next)
        return tuple(out)

    def diag_scores(i, buf):
        s_ref, _ = buf
        for hd in heads:
            qz = qz_ref.at[i % 2, hd]
            k_lo = k_ref[i * blk:i * blk + half, hd * hw:(hd + 1) * hw]
            k_hi = k_ref[i * blk + half:(i + 1) * blk, hd * hw:(hd + 1) * hw]
            s_ref[hd, :half, :] = _dot(k_lo, qz[...])
            s_ref[hd, half:, blk:] = _dot(k_hi, qz[:, blk:])

    def diag_accumulate(i, buf):
        s_ref, _ = buf
        key = lax.broadcasted_iota(jnp.int32, (half, blk), 0)
        qry = lax.broadcasted_iota(jnp.int32, (half, blk), 1)
        visible = key <= jnp.where(qry >= half, qry - half, qry)
        out = []
        for hd in heads:
            s_a = jnp.where(visible, s_ref[hd, :half, :blk], -jnp.inf)
            s_b = s_ref[hd, :half, blk:]
            s_c = jnp.where(visible, s_ref[hd, half:, blk:], -jnp.inf)
            m_lo = jnp.max(s_a, axis=0, keepdims=True)
            m_hi = jnp.maximum(jnp.max(s_b, axis=0, keepdims=True),
                               jnp.max(s_c, axis=0, keepdims=True))
            vt_lo = vt_ref[hd * VT_ROWS:(hd + 1) * VT_ROWS, i * blk:i * blk + half]
            vt_hi = vt_ref[hd * VT_ROWS:(hd + 1) * VT_ROWS, i * blk + half:(i + 1) * blk]
            acc_ref[i % 2, hd, :, :blk] = _dot(vt_lo, jnp.exp2(s_a - m_lo).astype(BF16))
            acc_ref[i % 2, hd, :, blk:] = (
                _dot(vt_lo, jnp.exp2(s_b - m_hi).astype(BF16))
                + _dot(vt_hi, jnp.exp2(s_c - m_hi).astype(BF16)))
            out.append(jnp.concatenate([m_lo, m_hi], axis=1))
        return tuple(out)

    def finish(i):
        for hd in heads:
            acc = acc_ref.at[i % 2, hd]
            for part in range(2):
                lo = part * blk
                o1 = acc[:V_DIM, lo:lo + half] / acc[V_DIM:V_DIM + 1, lo:lo + half]
                o2 = (acc[:V_DIM, lo + half:lo + blk]
                      / acc[V_DIM:V_DIM + 1, lo + half:lo + blk])
                o = o1 - lam * o2
                ms_o = jnp.mean(o * o, axis=0, keepdims=True)
                o = o * lax.rsqrt(ms_o + RMS_EPS) * sg_ref[...] * (1.0 - lambda_init)
                cols = slice(i * blk + part * half, i * blk + (part + 1) * half)
                o_ref[hd * V_DIM:(hd + 1) * V_DIM, cols] = o.astype(BF16)

    def start_next(i, buf):
        if i + 1 < n_tiles:
            load_queries(i + 1)
            diag_scores(i + 1, buf)

    load_queries(0)
    diag_scores(0, bufs[0])
    first = 0
    for i in range(n_tiles):
        b0, b1 = bufs[first], bufs[1 - first]
        if i == 0:
            start_next(i, b1)
            diag_accumulate(i, b0)
            first = 1 - first
            finish(i)
            continue

        def pair(t, ms, i=i, b0=b0, b1=b1):
            j = 2 * t
            scores(i, j + 1, b0)
            ms = accumulate(i, j, b1, ms)
            scores(i, j + 2, b1)
            return accumulate(i, j + 1, b0, ms)

        scores(i, 0, b1)
        ms = diag_accumulate(i, b0)
        if (i - 1) // 2 > 0:
            ms = lax.fori_loop(0, (i - 1) // 2, pair, ms)
        if i % 2 == 1:
            start_next(i, b0)
            accumulate(i, i - 1, b1, ms)
        else:
            scores(i, i - 1, b0)
            ms = accumulate(i, i - 2, b1, ms)
            start_next(i, b1)
            accumulate(i, i - 1, b0, ms)
            first = 1 - first
        finish(i)


def _attn(qt, k, vt, lam_params, subln_g, lambda_init, batch, seq):
    blk = ATTN_BLOCK
    nh = ATTN_HEADS
    qo = pl.BlockSpec((None, nh * V_DIM, seq), lambda b, h: (b, h, 0))
    return pl.pallas_call(
        functools.partial(_attn_kernel, lambda_init=lambda_init, n_tiles=seq // blk),
        grid=(batch, N_HEADS // nh),
        in_specs=[qo,
                  pl.BlockSpec((seq, nh * 2 * HEAD_DIM), lambda b, h: (b, h)),
                  pl.BlockSpec((None, nh * VT_ROWS, seq), lambda b, h: (b, h, 0)),
                  _resident((4, HEAD_DIM)), _resident((V_DIM, 1))],
        out_specs=qo,
        out_shape=jax.ShapeDtypeStruct((batch, N_HEADS * V_DIM, seq), BF16),
        scratch_shapes=[pltpu.VMEM((2, nh, 2 * HEAD_DIM, 2 * blk), BF16),
                        pltpu.VMEM((2, nh, VT_ROWS, 2 * blk), F32),
                        pltpu.VMEM((nh, blk, 2 * blk), F32),
                        pltpu.VMEM((nh, blk, 2 * blk), F32),
                        pltpu.VMEM((nh, 1, 2 * blk), F32),
                        pltpu.VMEM((nh, 1, 2 * blk), F32)],
        compiler_params=_params(),
        name="attn",
    )(qt, k, vt, lam_params, subln_g)


def _o_kernel(h_ref, ot_ref, wo_ref, out_ref):
    o = lax.dot_general(ot_ref[...], wo_ref[...], (((0,), (0,)), ((), ())),
                        preferred_element_type=F32)
    out_ref[...] = h_ref[...] + o


def _oproj(h, ot, wo, batch, seq):
    n = seq // PROJ_ROWS
    row = pl.BlockSpec((PROJ_ROWS, D_MODEL), lambda b, i: (b * n + i, 0))
    return pl.pallas_call(
        _o_kernel,
        grid=(batch, n),
        in_specs=[row,
                  pl.BlockSpec((None, N_HEADS * V_DIM, PROJ_ROWS), lambda b, i: (b, 0, i)),
                  _resident((N_HEADS * V_DIM, D_MODEL))],
        out_specs=row,
        out_shape=jax.ShapeDtypeStruct((batch * seq, D_MODEL), F32),
        compiler_params=_params(),
        name="oproj",
    )(h, ot, wo)


def kernel(x, norm_g, ffn_w_gate_up, ffn_w_down, a_w_in, a_b_in, a_ln_g, a_ln_b, a_w_s, a_b_s, a_w_out, a_b_out, kv_norm_g, w_kv, b_w_q, b_lambda, b_subln_g, b_w_o, final_norm_g):
    batch, seq, d = x.shape
    h = x.reshape(batch * seq, d)
    fg = final_norm_g.reshape(1, d)
    wgu = ffn_w_gate_up.astype(BF16)
    wd = ffn_w_down.astype(BF16)
    win = a_w_in.astype(BF16)
    wout = a_w_out.astype(BF16)
    k = vt = None
    for layer in range(DEPTH):
        h = _ffn(h, norm_g[layer, 0].reshape(1, d), wgu, wd, fg, (layer, 0), final_norm=False)
        g_mix = norm_g[layer, 1].reshape(1, d)
        if layer < N_A_LAYERS:
            a = layer
            h = _gmlp(h, g_mix, win, a_b_in[a].reshape(1, -1),
                      a_ln_g[a].reshape(1, -1), a_ln_b[a].reshape(1, -1), a_w_s[a],
                      a_b_s[a].T, wout, a_b_out[a].reshape(1, -1), (a,))
        else:
            j = layer - N_A_LAYERS
            lambda_init = 0.8 - 0.6 * math.exp(-0.3 * layer)
            qt = _qproj(h, g_mix, b_w_q[j].T.astype(BF16), batch, seq)
            ot = _attn(qt, k, vt, b_lambda[j], b_subln_g[j].reshape(-1, 1), lambda_init,
                       batch, seq)
            h = _oproj(h, ot, b_w_o[j].astype(BF16), batch, seq)
        h = _ffn(h, norm_g[layer, 2].reshape(1, d), wgu, wd, fg, (layer, 1),
                 final_norm=(layer == DEPTH - 1))
        if layer == N_A_LAYERS - 1:
            k, vt = _kv(h, kv_norm_g.reshape(1, d), w_kv[:, :K_WIDTH].astype(BF16),
                        w_kv[:, K_WIDTH:].T.astype(BF16), batch, seq)
    return h.reshape(batch, seq, d)
```

```python
import functools
import math

import jax
import jax.numpy as jnp
from jax import lax
from jax.experimental import pallas as pl
from jax.experimental.pallas import tpu as pltpu

D_MODEL = 1024
DEPTH = 4
N_A_LAYERS = DEPTH // 2
D_FF = 2816
D_GMLP = 6 * D_MODEL
D_GATE = D_GMLP // 2
SGU_GROUPS = 8
SGU_GROUP_DIM = D_GATE // SGU_GROUPS
CHUNK = 128
N_HEADS = 8
HEAD_DIM = 64
V_DIM = 128
K_WIDTH = N_HEADS * 2 * HEAD_DIM
RMS_EPS = 1e-6
LN_EPS = 1e-5

BF16 = jnp.bfloat16
F32 = jnp.float32

VMEM_LIMIT_BYTES = 56 * 1024 * 1024

FFN_ROWS = 1024
FFN_COLS = 256
GMLP_ROWS = 512
GMLP_COLS = 768
ATTN_BLOCK = 512
ATTN_HEADS = 2
ONES_ROWS = 16
VT_ROWS = V_DIM + ONES_ROWS
Q_SCALE = HEAD_DIM ** -0.5 * math.log2(math.e)


def _resident(shape):
    return pl.BlockSpec(shape, lambda *_: (0,) * len(shape),
                        pipeline_mode=pl.Buffered(1))


def _stacked(shape, index):
    lead = (None,) * len(index)
    return pl.BlockSpec(lead + shape, lambda *_: tuple(index) + (0,) * len(shape),
                        pipeline_mode=pl.Buffered(1))


def _params():
    return pltpu.CompilerParams(vmem_limit_bytes=VMEM_LIMIT_BYTES)


def _rms(x, g):
    return x * lax.rsqrt(jnp.mean(x * x, axis=-1, keepdims=True) + RMS_EPS) * g


def _gelu(z):
    return 0.5 * z * (1.0 + lax.erf(z * (2.0 ** -0.5)))


def _dot(a, b):
    return jnp.dot(a, b, preferred_element_type=F32)


def _swiglu_step(h, g_ref, wgu_ref, wd_ref, act_ref):
    xn = _rms(h, g_ref[...]).astype(BF16)
    for c in range(D_FF // FFN_COLS):
        lo = c * FFN_COLS
        gate = _dot(xn, wgu_ref[:, lo:lo + FFN_COLS])
        up = _dot(xn, wgu_ref[:, D_FF + lo:D_FF + lo + FFN_COLS])
        act = gate * (1.0 / (1.0 + jnp.exp(-gate))) * up
        act_ref[:, lo:lo + FFN_COLS] = act.astype(BF16)
    return h + 0.5 * _dot(act_ref[...], wd_ref[...])


def _dot_nt(w_t, x):
    return lax.dot_general(w_t, x, (((1,), (1,)), ((), ())), preferred_element_type=F32)


def _ffn_kernel(h_ref, g_ref, wgu_ref, wd_ref, o_ref, act_ref):
    o_ref[...] = _swiglu_step(h_ref[...], g_ref, wgu_ref, wd_ref, act_ref)


def _ffn_q_kernel(h_ref, g_ref, wgu_ref, wd_ref, gq_ref, wqt_ref, o_ref, qt_ref, act_ref):
    out = _swiglu_step(h_ref[...], g_ref, wgu_ref, wd_ref, act_ref)
    o_ref[...] = out
    xn = _rms(out, gq_ref[...]).astype(BF16)
    qt_ref[...] = (_dot_nt(wqt_ref[...], xn) * Q_SCALE).astype(BF16)


def _ffn_kv_kernel(h_ref, g_ref, wgu_ref, wd_ref, gk_ref, wk_ref, wvt_ref,
                   o_ref, k_ref, vt_ref, act_ref):
    out = _swiglu_step(h_ref[...], g_ref, wgu_ref, wd_ref, act_ref)
    o_ref[...] = out
    xn = _rms(out, gk_ref[...]).astype(BF16)
    k_ref[...] = _dot(xn, wk_ref[...]).astype(BF16)
    vt = _dot_nt(wvt_ref[...], xn).astype(BF16)
    ones = jnp.ones((ONES_ROWS, FFN_ROWS), BF16)
    for hd in range(N_HEADS):
        vt_ref[hd * VT_ROWS:hd * VT_ROWS + V_DIM, :] = vt[hd * V_DIM:(hd + 1) * V_DIM]
        vt_ref[hd * VT_ROWS + V_DIM:(hd + 1) * VT_ROWS, :] = ones


def _oproj_ffn_kernel(h_ref, ot_ref, wo_ref, g_ref, wgu_ref, wd_ref, fg_ref, o_ref, act_ref,
                      *, final_norm):
    h = h_ref[...] + lax.dot_general(ot_ref[...], wo_ref[...], (((0,), (0,)), ((), ())),
                                     preferred_element_type=F32)
    out = _swiglu_step(h, g_ref, wgu_ref, wd_ref, act_ref)
    if final_norm:
        out = _rms(out, fg_ref[...])
    o_ref[...] = out


def _ffn_call(body, name, seq, index, extra_in, extra_in_specs, extra_out_shapes,
              extra_out_specs, h, g, wgu, wd, lead_in=(), lead_in_specs=()):
    t = h.shape[0]
    row = pl.BlockSpec((FFN_ROWS, D_MODEL), lambda i: (i, 0))
    core_specs = [_resident((1, D_MODEL)), _stacked((D_MODEL, 2 * D_FF), index),
                  _stacked((D_FF, D_MODEL), index)]
    return pl.pallas_call(
        body,
        grid=(t // FFN_ROWS,),
        in_specs=[row, *lead_in_specs, *core_specs, *extra_in_specs],
        out_specs=[row, *extra_out_specs],
        out_shape=[jax.ShapeDtypeStruct((t, D_MODEL), F32), *extra_out_shapes],
        scratch_shapes=[pltpu.VMEM((FFN_ROWS, D_FF), BF16)],
        compiler_params=_params(),
        name=name,
    )(h, *lead_in, g, wgu, wd, *extra_in)


def _transposed_rows(channels, seq):
    n = seq // FFN_ROWS
    return pl.BlockSpec((None, channels, FFN_ROWS), lambda i: (i // n, 0, i % n))


def _ffn(h, g, wgu, wd, index, seq):
    return _ffn_call(_ffn_kernel, "ffn", seq, index, (), (), (), (), h, g, wgu, wd)[0]


def _ffn_q(h, g, wgu, wd, index, gq, wqt, batch, seq):
    return _ffn_call(
        _ffn_q_kernel, "ffn_q", seq, index,
        (gq, wqt), (_resident((1, D_MODEL)), _resident((D_MODEL, D_MODEL))),
        (jax.ShapeDtypeStruct((batch, D_MODEL, seq), BF16),),
        (_transposed_rows(D_MODEL, seq),), h, g, wgu, wd)


def _ffn_kv(h, g, wgu, wd, index, gk, wk, wvt, batch, seq):
    return _ffn_call(
        _ffn_kv_kernel, "ffn_kv", seq, index,
        (gk, wk, wvt),
        (_resident((1, D_MODEL)), _resident((D_MODEL, K_WIDTH)),
         _resident((N_HEADS * V_DIM, D_MODEL))),
        (jax.ShapeDtypeStruct((batch * seq, K_WIDTH), BF16),
         jax.ShapeDtypeStruct((batch, N_HEADS * VT_ROWS, seq), BF16)),
        (pl.BlockSpec((FFN_ROWS, K_WIDTH), lambda i: (i, 0)),
         _transposed_rows(N_HEADS * VT_ROWS, seq)), h, g, wgu, wd)


def _oproj_ffn(h, ot, wo, g, wgu, wd, fg, index, seq, final_norm):
    return _ffn_call(
        functools.partial(_oproj_ffn_kernel, final_norm=final_norm), "oproj_ffn", seq, index,
        (fg,), (_resident((1, D_MODEL)),), (), (), h, g, wgu, wd,
        lead_in=(ot, wo),
        lead_in_specs=(_transposed_rows(N_HEADS * V_DIM, seq),
                       _resident((N_HEADS * V_DIM, D_MODEL))))[0]


def _gmlp_kernel(h_ref, g_ref, win_ref, bin_ref, lng_ref, lnb_ref, ws_ref, bst_ref,
                 wout_ref, bout_ref, o_ref, v_ref, y_ref):
    h = h_ref[...]
    xn = _rms(h, g_ref[...]).astype(BF16)
    gd = SGU_GROUP_DIM
    cw = GMLP_COLS

    s1 = jnp.zeros((GMLP_ROWS, 1), F32)
    s2 = jnp.zeros((GMLP_ROWS, 1), F32)
    for c in range(D_GATE // cw):
        lo = D_GATE + c * cw
        v = _gelu(_dot(xn, win_ref[:, lo:lo + cw]) + bin_ref[:, lo:lo + cw])
        v_ref[:, c * cw:(c + 1) * cw] = v
        s1 = s1 + jnp.sum(v, axis=-1, keepdims=True)
        s2 = s2 + jnp.sum(v * v, axis=-1, keepdims=True)
    mu = s1 * (1.0 / D_GATE)
    var = s2 * (1.0 / D_GATE) - mu * mu
    rstd = lax.rsqrt(var + LN_EPS)

    rows = lax.broadcasted_iota(jnp.int32, (CHUNK, CHUNK), 0)
    cols = lax.broadcasted_iota(jnp.int32, (CHUNK, CHUNK), 1)
    causal = rows >= cols

    for c in range(D_GATE // cw):
        lo = c * cw
        u = _gelu(_dot(xn, win_ref[:, lo:lo + cw]) + bin_ref[:, lo:lo + cw])
        for g in range(c * (cw // gd), (c + 1) * (cw // gd)):
            glo = g * gd
            vn = ((v_ref[:, glo:glo + gd] - mu) * rstd * lng_ref[:, glo:glo + gd]
                  + lnb_ref[:, glo:glo + gd]).astype(BF16)
            ws = jnp.where(causal, ws_ref[g], 0.0).astype(BF16)
            bias = bst_ref[:, g:g + 1]
            s = jnp.concatenate(
                [_dot(ws, vn[r * CHUNK:(r + 1) * CHUNK]) + bias
                 for r in range(GMLP_ROWS // CHUNK)], axis=0)
            y_ref[:, glo:glo + gd] = (u[:, glo - lo:glo - lo + gd] * s).astype(BF16)

    o_ref[...] = h + _dot(y_ref[...], wout_ref[...]) + bout_ref[...]


def _gmlp(h, g, win, b_in, ln_g, ln_b, ws, bst, wout, b_out, index):
    t = h.shape[0]
    row = pl.BlockSpec((GMLP_ROWS, D_MODEL), lambda i: (i, 0))
    return pl.pallas_call(
        _gmlp_kernel,
        grid=(t // GMLP_ROWS,),
        in_specs=[row, _resident((1, D_MODEL)), _stacked((D_MODEL, D_GMLP), index),
                  _resident((1, D_GMLP)), _resident((1, D_GATE)), _resident((1, D_GATE)),
                  _resident((SGU_GROUPS, CHUNK, CHUNK)), _resident((CHUNK, SGU_GROUPS)),
                  _stacked((D_GATE, D_MODEL), index), _resident((1, D_MODEL))],
        out_specs=row,
        out_shape=jax.ShapeDtypeStruct((t, D_MODEL), F32),
        scratch_shapes=[pltpu.VMEM((GMLP_ROWS, D_GATE), F32),
                        pltpu.VMEM((GMLP_ROWS, D_GATE), BF16)],
        compiler_params=_params(),
        name="gmlp",
    )(h, g, win, b_in, ln_g, ln_b, ws, bst, wout, b_out)


def _attn_kernel(qt_ref, k_ref, vt_ref, lam_ref, sg_ref, o_ref, qz_ref, acc_ref,
                 sa_ref, sb_ref, mxa_ref, mxb_ref, *, lambda_init, n_tiles):
    blk = ATTN_BLOCK
    half = blk // 2
    hw = 2 * HEAD_DIM
    heads = range(ATTN_HEADS)
    bufs = ((sa_ref, mxa_ref), (sb_ref, mxb_ref))

    lp = lam_ref[...]
    lam = (jnp.exp(jnp.sum(lp[0:1] * lp[1:2], axis=-1, keepdims=True))
           - jnp.exp(jnp.sum(lp[2:3] * lp[3:4], axis=-1, keepdims=True))
           + lambda_init)

    def block_start(j):
        return j * blk if isinstance(j, int) else pl.multiple_of(j * blk, blk)

    def load_queries(i):
        zeros = jnp.zeros((HEAD_DIM, half), BF16)
        for hd in heads:
            for part in range(2):
                cols = slice(i * blk + part * half, i * blk + (part + 1) * half)
                lo = part * blk
                q1 = qt_ref[hd * hw:hd * hw + HEAD_DIM, cols]
                q2 = qt_ref[hd * hw + HEAD_DIM:(hd + 1) * hw, cols]
                qz_ref[i % 2, hd, :HEAD_DIM, lo:lo + half] = q1
                qz_ref[i % 2, hd, HEAD_DIM:, lo:lo + half] = zeros
                qz_ref[i % 2, hd, :HEAD_DIM, lo + half:lo + blk] = zeros
                qz_ref[i % 2, hd, HEAD_DIM:, lo + half:lo + blk] = q2

    def scores(i, j, buf):
        s_ref, mx_ref = buf
        k = k_ref[pl.ds(block_start(j), blk), :]
        for hd in heads:
            s = _dot(k[:, hd * hw:(hd + 1) * hw], qz_ref[i % 2, hd])
            s_ref[hd] = s
            mx_ref[hd] = jnp.max(s, axis=0, keepdims=True)

    def accumulate(i, j, buf, ms):
        s_ref, mx_ref = buf
        out = []
        for hd in heads:
            m_next = jnp.maximum(ms[hd], mx_ref[hd])
            p = jnp.exp2(s_ref[hd] - m_next).astype(BF16)
            alpha = jnp.exp2(ms[hd] - m_next)
            vt = vt_ref[hd * VT_ROWS:(hd + 1) * VT_ROWS, pl.ds(block_start(j), blk)]
            acc_ref[i % 2, hd] = alpha * acc_ref[i % 2, hd] + _dot(vt, p)
            out.append(m_next)
        return tuple(out)

    def diag_scores(i, buf):
        s_ref, _ = buf
        for hd in heads:
            qz = qz_ref.at[i % 2, hd]
            k_lo = k_ref[i * blk:i * blk + half, hd * hw:(hd + 1) * hw]
            k_hi = k_ref[i * blk + half:(i + 1) * blk, hd * hw:(hd + 1) * hw]
            s_ref[hd, :half, :] = _dot(k_lo, qz[...])
            s_ref[hd, half:, blk:] = _dot(k_hi, qz[:, blk:])

    def diag_accumulate(i, buf):
        s_ref, _ = buf
        key = lax.broadcasted_iota(jnp.int32, (half, blk), 0)
        qry = lax.broadcasted_iota(jnp.int32, (half, blk), 1)
        visible = key <= jnp.where(qry >= half, qry - half, qry)
        out = []
        for hd in heads:
            s_a = jnp.where(visible, s_ref[hd, :half, :blk], -jnp.inf)
            s_b = s_ref[hd, :half, blk:]
            s_c = jnp.where(visible, s_ref[hd, half:, blk:], -jnp.inf)
            m_lo = jnp.max(s_a, axis=0, keepdims=True)
            m_hi = jnp.maximum(jnp.max(s_b, axis=0, keepdims=True),
                               jnp.max(s_c, axis=0, keepdims=True))
            vt_lo = vt_ref[hd * VT_ROWS:(hd + 1) * VT_ROWS, i * blk:i * blk + half]
            vt_hi = vt_ref[hd * VT_ROWS:(hd + 1) * VT_ROWS, i * blk + half:(i + 1) * blk]
            acc_ref[i % 2, hd, :, :blk] = _dot(vt_lo, jnp.exp2(s_a - m_lo).astype(BF16))
            acc_ref[i % 2, hd, :, blk:] = (
                _dot(vt_lo, jnp.exp2(s_b - m_hi).astype(BF16))
                + _dot(vt_hi, jnp.exp2(s_c - m_hi).astype(BF16)))
            out.append(jnp.concatenate([m_lo, m_hi], axis=1))
        return tuple(out)

    def finish(i):
        for hd in heads:
            acc = acc_ref.at[i % 2, hd]
            for part in range(2):
                lo = part * blk
                o1 = acc[:V_DIM, lo:lo + half] / acc[V_DIM:V_DIM + 1, lo:lo + half]
                o2 = (acc[:V_DIM, lo + half:lo + blk]
                      / acc[V_DIM:V_DIM + 1, lo + half:lo + blk])
                o = o1 - lam * o2
                ms_o = jnp.mean(o * o, axis=0, keepdims=True)
                o = o * lax.rsqrt(ms_o + RMS_EPS) * sg_ref[...] * (1.0 - lambda_init)
                cols = slice(i * blk + part * half, i * blk + (part + 1) * half)
                o_ref[hd * V_DIM:(hd + 1) * V_DIM, cols] = o.astype(BF16)

    def start_next(i, buf):
        if i + 1 < n_tiles:
            load_queries(i + 1)
            diag_scores(i + 1, buf)

    load_queries(0)
    diag_scores(0, bufs[0])
    first = 0
    for i in range(n_tiles):
        b0, b1 = bufs[first], bufs[1 - first]
        if i == 0:
            start_next(i, b1)
            diag_accumulate(i, b0)
            first = 1 - first
            finish(i)
            continue

        def pair(t, ms, i=i, b0=b0, b1=b1):
            j = 2 * t
            scores(i, j + 1, b0)
            ms = accumulate(i, j, b1, ms)
            scores(i, j + 2, b1)
            return accumulate(i, j + 1, b0, ms)

        scores(i, 0, b1)
        ms = diag_accumulate(i, b0)
        if (i - 1) // 2 > 0:
            ms = lax.fori_loop(0, (i - 1) // 2, pair, ms)
        if i % 2 == 1:
            start_next(i, b0)
            accumulate(i, i - 1, b1, ms)
        else:
            scores(i, i - 1, b0)
            ms = accumulate(i, i - 2, b1, ms)
            start_next(i, b1)
            accumulate(i, i - 1, b0, ms)
            first = 1 - first
        finish(i)


def _attn(qt, k, vt, lam_params, subln_g, lambda_init, batch, seq):
    blk = ATTN_BLOCK
    nh = ATTN_HEADS
    qo = pl.BlockSpec((None, nh * V_DIM, seq), lambda b, h: (b, h, 0))
    return pl.pallas_call(
        functools.partial(_attn_kernel, lambda_init=lambda_init, n_tiles=seq // blk),
        grid=(batch, N_HEADS // nh),
        in_specs=[qo,
                  pl.BlockSpec((seq, nh * 2 * HEAD_DIM), lambda b, h: (b, h)),
                  pl.BlockSpec((None, nh * VT_ROWS, seq), lambda b, h: (b, h, 0)),
                  _resident((4, HEAD_DIM)), _resident((V_DIM, 1))],
        out_specs=qo,
        out_shape=jax.ShapeDtypeStruct((batch, N_HEADS * V_DIM, seq), BF16),
        scratch_shapes=[pltpu.VMEM((2, nh, 2 * HEAD_DIM, 2 * blk), BF16),
                        pltpu.VMEM((2, nh, VT_ROWS, 2 * blk), F32),
                        pltpu.VMEM((nh, blk, 2 * blk), F32),
                        pltpu.VMEM((nh, blk, 2 * blk), F32),
                        pltpu.VMEM((nh, 1, 2 * blk), F32),
                        pltpu.VMEM((nh, 1, 2 * blk), F32)],
        compiler_params=_params(),
        name="attn",
    )(qt, k, vt, lam_params, subln_g)


def kernel(x, norm_g, ffn_w_gate_up, ffn_w_down, a_w_in, a_b_in, a_ln_g, a_ln_b, a_w_s, a_b_s, a_w_out, a_b_out, kv_norm_g, w_kv, b_w_q, b_lambda, b_subln_g, b_w_o, final_norm_g):
    batch, seq, d = x.shape
    h = x.reshape(batch * seq, d)
    fg = final_norm_g.reshape(1, d)
    wgu = ffn_w_gate_up.astype(BF16)
    wd = ffn_w_down.astype(BF16)
    win = a_w_in.astype(BF16)
    wout = a_w_out.astype(BF16)
    k = vt = None
    for layer in range(DEPTH):
        g0 = norm_g[layer, 0].reshape(1, d)
        g_mix = norm_g[layer, 1].reshape(1, d)
        g1 = norm_g[layer, 2].reshape(1, d)
        if layer < N_A_LAYERS:
            a = layer
            h = _ffn(h, g0, wgu, wd, (layer, 0), seq)
            h = _gmlp(h, g_mix, win, a_b_in[a].reshape(1, -1),
                      a_ln_g[a].reshape(1, -1), a_ln_b[a].reshape(1, -1), a_w_s[a],
                      a_b_s[a].T, wout, a_b_out[a].reshape(1, -1), (a,))
            if layer < N_A_LAYERS - 1:
                h = _ffn(h, g1, wgu, wd, (layer, 1), seq)
            else:
                h, k, vt = _ffn_kv(h, g1, wgu, wd, (layer, 1), kv_norm_g.reshape(1, d),
                                   w_kv[:, :K_WIDTH].astype(BF16),
                                   w_kv[:, K_WIDTH:].T.astype(BF16), batch, seq)
        else:
            j = layer - N_A_LAYERS
            lambda_init = 0.8 - 0.6 * math.exp(-0.3 * layer)
            h, qt = _ffn_q(h, g0, wgu, wd, (layer, 0), g_mix, b_w_q[j].T.astype(BF16),
                           batch, seq)
            ot = _attn(qt, k, vt, b_lambda[j], b_subln_g[j].reshape(-1, 1), lambda_init,
                       batch, seq)
            h = _oproj_ffn(h, ot, b_w_o[j].astype(BF16), g1, wgu, wd, fg, (layer, 1), seq,
                           final_norm=(layer == DEPTH - 1))
    return h.reshape(batch, seq, d)
```

```python
import functools
import math

import jax
import jax.numpy as jnp
from jax import lax
from jax.experimental import pallas as pl
from jax.experimental.pallas import tpu as pltpu

D_MODEL = 1024
DEPTH = 4
N_A_LAYERS = DEPTH // 2
D_FF = 2816
D_GMLP = 6 * D_MODEL
D_GATE = D_GMLP // 2
SGU_GROUPS = 8
SGU_GROUP_DIM = D_GATE // SGU_GROUPS
CHUNK = 128
N_HEADS = 8
HEAD_DIM = 64
V_DIM = 128
K_WIDTH = N_HEADS * 2 * HEAD_DIM
RMS_EPS = 1e-6
LN_EPS = 1e-5

BF16 = jnp.bfloat16
F32 = jnp.float32

VMEM_LIMIT_BYTES = 56 * 1024 * 1024

FFN_ROWS = 1024
FFN_COLS = 256
FFN_STREAMS = 2
GMLP_ROWS = 512
GMLP_COLS = 768
ATTN_BLOCK = 512
ATTN_HEADS = 2
ONES_ROWS = 16
VT_ROWS = V_DIM + ONES_ROWS
Q_SCALE = HEAD_DIM ** -0.5 * math.log2(math.e)


def _resident(shape):
    return pl.BlockSpec(shape, lambda *_: (0,) * len(shape),
                        pipeline_mode=pl.Buffered(1))


def _stacked(shape, index):
    lead = (None,) * len(index)
    return pl.BlockSpec(lead + shape, lambda *_: tuple(index) + (0,) * len(shape),
                        pipeline_mode=pl.Buffered(1))


def _params():
    return pltpu.CompilerParams(vmem_limit_bytes=VMEM_LIMIT_BYTES)


def _rms(x, g):
    return x * lax.rsqrt(jnp.mean(x * x, axis=-1, keepdims=True) + RMS_EPS) * g


def _gelu_x2(z):
    return z * (1.0 + lax.erf(z * (2.0 ** -0.5)))


def _dot(a, b):
    return jnp.dot(a, b, preferred_element_type=F32)


def _swiglu_step(h, g_ref, wgu_ref, wd_ref, act_ref):
    rows = h.shape[0] // FFN_STREAMS
    outs = []
    for r in range(FFN_STREAMS):
        hr = h[r * rows:(r + 1) * rows]
        xn = _rms(hr, g_ref[...]).astype(BF16)
        for c in range(D_FF // FFN_COLS):
            lo = c * FFN_COLS
            gate = _dot(xn, wgu_ref[:, lo:lo + FFN_COLS])
            up = _dot(xn, wgu_ref[:, D_FF + lo:D_FF + lo + FFN_COLS])
            act = gate * (1.0 / (1.0 + jnp.exp(-gate))) * up
            act_ref[r * rows:(r + 1) * rows, lo:lo + FFN_COLS] = act.astype(BF16)
        outs.append(hr + 0.5 * _dot(act_ref[r * rows:(r + 1) * rows, :], wd_ref[...]))
    return jnp.concatenate(outs, axis=0)


def _dot_nt(w_t, x):
    return lax.dot_general(w_t, x, (((1,), (1,)), ((), ())), preferred_element_type=F32)


def _ffn_kernel(h_ref, g_ref, wgu_ref, wd_ref, o_ref, act_ref):
    o_ref[...] = _swiglu_step(h_ref[...], g_ref, wgu_ref, wd_ref, act_ref)


def _ffn_q_kernel(h_ref, g_ref, wgu_ref, wd_ref, gq_ref, wqt_ref, o_ref, qt_ref, act_ref):
    out = _swiglu_step(h_ref[...], g_ref, wgu_ref, wd_ref, act_ref)
    o_ref[...] = out
    xn = _rms(out, gq_ref[...]).astype(BF16)
    qt_ref[...] = (_dot_nt(wqt_ref[...], xn) * Q_SCALE).astype(BF16)


def _ffn_kv_kernel(h_ref, g_ref, wgu_ref, wd_ref, gk_ref, wk_ref, wvt_ref,
                   o_ref, k_ref, vt_ref, act_ref):
    out = _swiglu_step(h_ref[...], g_ref, wgu_ref, wd_ref, act_ref)
    o_ref[...] = out
    xn = _rms(out, gk_ref[...]).astype(BF16)
    k_ref[...] = _dot(xn, wk_ref[...]).astype(BF16)
    vt = _dot_nt(wvt_ref[...], xn).astype(BF16)
    ones = jnp.ones((ONES_ROWS, FFN_ROWS), BF16)
    for hd in range(N_HEADS):
        vt_ref[hd * VT_ROWS:hd * VT_ROWS + V_DIM, :] = vt[hd * V_DIM:(hd + 1) * V_DIM]
        vt_ref[hd * VT_ROWS + V_DIM:(hd + 1) * VT_ROWS, :] = ones


def _oproj_ffn_kernel(h_ref, ot_ref, wo_ref, g_ref, wgu_ref, wd_ref, fg_ref, o_ref, act_ref,
                      *, final_norm):
    h = h_ref[...] + lax.dot_general(ot_ref[...], wo_ref[...], (((0,), (0,)), ((), ())),
                                     preferred_element_type=F32)
    out = _swiglu_step(h, g_ref, wgu_ref, wd_ref, act_ref)
    if final_norm:
        out = _rms(out, fg_ref[...])
    o_ref[...] = out


def _ffn_call(body, name, seq, index, extra_in, extra_in_specs, extra_out_shapes,
              extra_out_specs, h, g, wgu, wd, lead_in=(), lead_in_specs=()):
    t = h.shape[0]
    row = pl.BlockSpec((FFN_ROWS, D_MODEL), lambda i: (i, 0))
    core_specs = [_resident((1, D_MODEL)), _stacked((D_MODEL, 2 * D_FF), index),
                  _stacked((D_FF, D_MODEL), index)]
    return pl.pallas_call(
        body,
        grid=(t // FFN_ROWS,),
        in_specs=[row, *lead_in_specs, *core_specs, *extra_in_specs],
        out_specs=[row, *extra_out_specs],
        out_shape=[jax.ShapeDtypeStruct((t, D_MODEL), F32), *extra_out_shapes],
        scratch_shapes=[pltpu.VMEM((FFN_ROWS, D_FF), BF16)],
        compiler_params=_params(),
        name=name,
    )(h, *lead_in, g, wgu, wd, *extra_in)


def _transposed_rows(channels, seq):
    n = seq // FFN_ROWS
    return pl.BlockSpec((None, channels, FFN_ROWS), lambda i: (i // n, 0, i % n))


def _ffn(h, g, wgu, wd, index, seq):
    return _ffn_call(_ffn_kernel, "ffn", seq, index, (), (), (), (), h, g, wgu, wd)[0]


def _ffn_q(h, g, wgu, wd, index, gq, wqt, batch, seq):
    return _ffn_call(
        _ffn_q_kernel, "ffn_q", seq, index,
        (gq, wqt), (_resident((1, D_MODEL)), _resident((D_MODEL, D_MODEL))),
        (jax.ShapeDtypeStruct((batch, D_MODEL, seq), BF16),),
        (_transposed_rows(D_MODEL, seq),), h, g, wgu, wd)


def _ffn_kv(h, g, wgu, wd, index, gk, wk, wvt, batch, seq):
    return _ffn_call(
        _ffn_kv_kernel, "ffn_kv", seq, index,
        (gk, wk, wvt),
        (_resident((1, D_MODEL)), _resident((D_MODEL, K_WIDTH)),
         _resident((N_HEADS * V_DIM, D_MODEL))),
        (jax.ShapeDtypeStruct((batch * seq, K_WIDTH), BF16),
         jax.ShapeDtypeStruct((batch, N_HEADS * VT_ROWS, seq), BF16)),
        (pl.BlockSpec((FFN_ROWS, K_WIDTH), lambda i: (i, 0)),
         _transposed_rows(N_HEADS * VT_ROWS, seq)), h, g, wgu, wd)


def _oproj_ffn(h, ot, wo, g, wgu, wd, fg, index, seq, final_norm):
    return _ffn_call(
        functools.partial(_oproj_ffn_kernel, final_norm=final_norm), "oproj_ffn", seq, index,
        (fg,), (_resident((1, D_MODEL)),), (), (), h, g, wgu, wd,
        lead_in=(ot, wo),
        lead_in_specs=(_transposed_rows(N_HEADS * V_DIM, seq),
                       _resident((N_HEADS * V_DIM, D_MODEL))))[0]


def _gmlp_kernel(h_ref, g_ref, win_ref, bin_ref, lng_ref, lnb_ref, ws_ref, bst_ref,
                 wout_ref, bout_ref, o_ref, v_ref, y_ref):
    h = h_ref[...]
    xn = _rms(h, g_ref[...]).astype(BF16)
    gd = SGU_GROUP_DIM
    cw = GMLP_COLS

    n_chunks = GMLP_ROWS // CHUNK

    s1 = jnp.zeros((GMLP_ROWS, 1), F32)
    s2 = jnp.zeros((GMLP_ROWS, 1), F32)
    for c in range(D_GATE // cw):
        lo = D_GATE + c * cw
        v = _gelu_x2(_dot(xn, win_ref[:, lo:lo + cw]) + bin_ref[:, lo:lo + cw])
        v_ref[:, c * cw:(c + 1) * cw] = v
        s1 = s1 + jnp.sum(v, axis=-1, keepdims=True)
        s2 = s2 + jnp.sum(v * v, axis=-1, keepdims=True)
    mu = s1 * (1.0 / D_GATE)
    var = s2 * (1.0 / D_GATE) - mu * mu
    rstd = lax.rsqrt(var + 4.0 * LN_EPS)

    rows = lax.broadcasted_iota(jnp.int32, (CHUNK, CHUNK), 0)
    cols = lax.broadcasted_iota(jnp.int32, (CHUNK, CHUNK), 1)
    causal = rows >= cols

    for c in range(D_GATE // cw):
        lo = c * cw
        u = _gelu_x2(_dot(xn, win_ref[:, lo:lo + cw]) + bin_ref[:, lo:lo + cw])
        for g in range(c * (cw // gd), (c + 1) * (cw // gd)):
            glo = g * gd
            vn = ((v_ref[:, glo:glo + gd] - mu) * rstd * lng_ref[:, glo:glo + gd]
                  + lnb_ref[:, glo:glo + gd]).astype(BF16)
            ws = jnp.where(causal, 0.5 * ws_ref[g], 0.0).astype(BF16)
            bias = 0.5 * bst_ref[:, g:g + 1]
            wide = jnp.concatenate(
                [vn[r * CHUNK:(r + 1) * CHUNK] for r in range(n_chunks)], axis=1)
            s_wide = _dot(ws, wide) + bias
            s = jnp.concatenate(
                [s_wide[:, r * gd:(r + 1) * gd] for r in range(n_chunks)], axis=0)
            y_ref[:, glo:glo + gd] = (u[:, glo - lo:glo - lo + gd] * s).astype(BF16)

    o_ref[...] = h + _dot(y_ref[...], wout_ref[...]) + bout_ref[...]


def _gmlp(h, g, win, b_in, ln_g, ln_b, ws, bst, wout, b_out, index):
    t = h.shape[0]
    row = pl.BlockSpec((GMLP_ROWS, D_MODEL), lambda i: (i, 0))
    return pl.pallas_call(
        _gmlp_kernel,
        grid=(t // GMLP_ROWS,),
        in_specs=[row, _resident((1, D_MODEL)), _stacked((D_MODEL, D_GMLP), index),
                  _resident((1, D_GMLP)), _resident((1, D_GATE)), _resident((1, D_GATE)),
                  _resident((SGU_GROUPS, CHUNK, CHUNK)), _resident((CHUNK, SGU_GROUPS)),
                  _stacked((D_GATE, D_MODEL), index), _resident((1, D_MODEL))],
        out_specs=row,
        out_shape=jax.ShapeDtypeStruct((t, D_MODEL), F32),
        scratch_shapes=[pltpu.VMEM((GMLP_ROWS, D_GATE), F32),
                        pltpu.VMEM((GMLP_ROWS, D_GATE), BF16)],
        compiler_params=_params(),
        name="gmlp",
    )(h, g, win, b_in, ln_g, ln_b, ws, bst, wout, b_out)


def _attn_kernel(qt_ref, k_ref, vt_ref, lam_ref, sg_ref, o_ref, qz_ref, acc_ref,
                 sa_ref, sb_ref, mxa_ref, mxb_ref, *, lambda_init, n_tiles):
    blk = ATTN_BLOCK
    half = blk // 2
    hw = 2 * HEAD_DIM
    heads = range(ATTN_HEADS)
    bufs = ((sa_ref, mxa_ref), (sb_ref, mxb_ref))

    lp = lam_ref[...]
    lam = (jnp.exp(jnp.sum(lp[0:1] * lp[1:2], axis=-1, keepdims=True))
           - jnp.exp(jnp.sum(lp[2:3] * lp[3:4], axis=-1, keepdims=True))
           + lambda_init)

    def block_start(j):
        return j * blk if isinstance(j, int) else pl.multiple_of(j * blk, blk)

    def load_queries(i):
        zeros = jnp.zeros((HEAD_DIM, half), BF16)
        for hd in heads:
            for part in range(2):
                cols = slice(i * blk + part * half, i * blk + (part + 1) * half)
                lo = part * blk
                q1 = qt_ref[hd * hw:hd * hw + HEAD_DIM, cols]
                q2 = qt_ref[hd * hw + HEAD_DIM:(hd + 1) * hw, cols]
                qz_ref[i % 2, hd, :HEAD_DIM, lo:lo + half] = q1
                qz_ref[i % 2, hd, HEAD_DIM:, lo:lo + half] = zeros
                qz_ref[i % 2, hd, :HEAD_DIM, lo + half:lo + blk] = zeros
                qz_ref[i % 2, hd, HEAD_DIM:, lo + half:lo + blk] = q2

    def scores(i, j, buf):
        s_ref, mx_ref = buf
        k = k_ref[pl.ds(block_start(j), blk), :]
        for hd in heads:
            s = _dot(k[:, hd * hw:(hd + 1) * hw], qz_ref[i % 2, hd])
            s_ref[hd] = s
            mx_ref[hd] = jnp.max(s, axis=0, keepdims=True)

    def accumulate(i, j, buf, ms):
        s_ref, mx_ref = buf
        out = []
        for hd in heads:
            m_next = jnp.maximum(ms[hd], mx_ref[hd])
            p = jnp.exp2(s_ref[hd] - m_next).astype(BF16)
            alpha = jnp.exp2(ms[hd] - m_next)
            vt = vt_ref[hd * VT_ROWS:(hd + 1) * VT_ROWS, pl.ds(block_start(j), blk)]
            acc_ref[i % 2, hd] = alpha * acc_ref[i % 2, hd] + _dot(vt, p)
            out.append(m_next)
        return tuple(out)

    def diag_scores(i, buf):
        s_ref, _ = buf
        for hd in heads:
            qz = qz_ref.at[i % 2, hd]
            k_lo = k_ref[i * blk:i * blk + half, hd * hw:(hd + 1) * hw]
            k_hi = k_ref[i * blk + half:(i + 1) * blk, hd * hw:(hd + 1) * hw]
            s_ref[hd, :half, :] = _dot(k_lo, qz[...])
            s_ref[hd, half:, blk:] = _dot(k_hi, qz[:, blk:])

    def diag_accumulate(i, buf):
        s_ref, _ = buf
        key = lax.broadcasted_iota(jnp.int32, (half, blk), 0)
        qry = lax.broadcasted_iota(jnp.int32, (half, blk), 1)
        visible = key <= jnp.where(qry >= half, qry - half, qry)
        out = []
        for hd in heads:
            s_a = jnp.where(visible, s_ref[hd, :half, :blk], -jnp.inf)
            s_b = s_ref[hd, :half, blk:]
            s_c = jnp.where(visible, s_ref[hd, half:, blk:], -jnp.inf)
            m_lo = jnp.max(s_a, axis=0, keepdims=True)
            m_hi = jnp.maximum(jnp.max(s_b, axis=0, keepdims=True),
                               jnp.max(s_c, axis=0, keepdims=True))
            vt_lo = vt_ref[hd * VT_ROWS:(hd + 1) * VT_ROWS, i * blk:i * blk + half]
            vt_hi = vt_ref[hd * VT_ROWS:(hd + 1) * VT_ROWS, i * blk + half:(i + 1) * blk]
            acc_ref[i % 2, hd, :, :blk] = _dot(vt_lo, jnp.exp2(s_a - m_lo).astype(BF16))
            acc_ref[i % 2, hd, :, blk:] = (
                _dot(vt_lo, jnp.exp2(s_b - m_hi).astype(BF16))
                + _dot(vt_hi, jnp.exp2(s_c - m_hi).astype(BF16)))
            out.append(jnp.concatenate([m_lo, m_hi], axis=1))
        return tuple(out)

    def finish(i):
        for hd in heads:
            acc = acc_ref.at[i % 2, hd]
            for part in range(2):
                lo = part * blk
                o1 = acc[:V_DIM, lo:lo + half] / acc[V_DIM:V_DIM + 1, lo:lo + half]
                o2 = (acc[:V_DIM, lo + half:lo + blk]
                      / acc[V_DIM:V_DIM + 1, lo + half:lo + blk])
                o = o1 - lam * o2
                ms_o = jnp.mean(o * o, axis=0, keepdims=True)
                o = o * lax.rsqrt(ms_o + RMS_EPS) * sg_ref[...] * (1.0 - lambda_init)
                cols = slice(i * blk + part * half, i * blk + (part + 1) * half)
                o_ref[hd * V_DIM:(hd + 1) * V_DIM, cols] = o.astype(BF16)

    def start_next(i, buf):
        if i + 1 < n_tiles:
            load_queries(i + 1)
            diag_scores(i + 1, buf)

    load_queries(0)
    diag_scores(0, bufs[0])
    first = 0
    for i in range(n_tiles):
        b0, b1 = bufs[first], bufs[1 - first]
        if i == 0:
            start_next(i, b1)
            diag_accumulate(i, b0)
            first = 1 - first
            finish(i)
            continue

        def pair(t, ms, i=i, b0=b0, b1=b1):
            j = 2 * t
            scores(i, j + 1, b0)
            ms = accumulate(i, j, b1, ms)
            scores(i, j + 2, b1)
            return accumulate(i, j + 1, b0, ms)

        scores(i, 0, b1)
        ms = diag_accumulate(i, b0)
        if (i - 1) // 2 > 0:
            ms = lax.fori_loop(0, (i - 1) // 2, pair, ms)
        if i % 2 == 1:
            start_next(i, b0)
            accumulate(i, i - 1, b1, ms)
        else:
            scores(i, i - 1, b0)
            ms = accumulate(i, i - 2, b1, ms)
            start_next(i, b1)
            accumulate(i, i - 1, b0, ms)
            first = 1 - first
        finish(i)


def _attn(qt, k, vt, lam_params, subln_g, lambda_init, batch, seq):
    blk = ATTN_BLOCK
    nh = ATTN_HEADS
    qo = pl.BlockSpec((None, nh * V_DIM, seq), lambda b, h: (b, h, 0))
    return pl.pallas_call(
        functools.partial(_attn_kernel, lambda_init=lambda_init, n_tiles=seq // blk),
        grid=(batch, N_HEADS // nh),
        in_specs=[qo,
                  pl.BlockSpec((seq, nh * 2 * HEAD_DIM), lambda b, h: (b, h)),
                  pl.BlockSpec((None, nh * VT_ROWS, seq), lambda b, h: (b, h, 0)),
                  _resident((4, HEAD_DIM)), _resident((V_DIM, 1))],
        out_specs=qo,
        out_shape=jax.ShapeDtypeStruct((batch, N_HEADS * V_DIM, seq), BF16),
        scratch_shapes=[pltpu.VMEM((2, nh, 2 * HEAD_DIM, 2 * blk), BF16),
                        pltpu.VMEM((2, nh, VT_ROWS, 2 * blk), F32),
                        pltpu.VMEM((nh, blk, 2 * blk), F32),
                        pltpu.VMEM((nh, blk, 2 * blk), F32),
                        pltpu.VMEM((nh, 1, 2 * blk), F32),
                        pltpu.VMEM((nh, 1, 2 * blk), F32)],
        compiler_params=_params(),
        name="attn",
    )(qt, k, vt, lam_params, subln_g)


def kernel(x, norm_g, ffn_w_gate_up, ffn_w_down, a_w_in, a_b_in, a_ln_g, a_ln_b, a_w_s, a_b_s, a_w_out, a_b_out, kv_norm_g, w_kv, b_w_q, b_lambda, b_subln_g, b_w_o, final_norm_g):
    batch, seq, d = x.shape
    h = x.reshape(batch * seq, d)
    fg = final_norm_g.reshape(1, d)
    wgu = ffn_w_gate_up.astype(BF16)
    wd = ffn_w_down.astype(BF16)
    win = a_w_in.astype(BF16)
    wout = a_w_out.astype(BF16)
    k = vt = None
    for layer in range(DEPTH):
        g0 = norm_g[layer, 0].reshape(1, d)
        g_mix = norm_g[layer, 1].reshape(1, d)
        g1 = norm_g[layer, 2].reshape(1, d)
        if layer < N_A_LAYERS:
            a = layer
            h = _ffn(h, g0, wgu, wd, (layer, 0), seq)
            h = _gmlp(h, g_mix, win, a_b_in[a].reshape(1, -1),
                      a_ln_g[a].reshape(1, -1), a_ln_b[a].reshape(1, -1), a_w_s[a],
                      a_b_s[a].T, wout, a_b_out[a].reshape(1, -1), (a,))
            if layer < N_A_LAYERS - 1:
                h = _ffn(h, g1, wgu, wd, (layer, 1), seq)
            else:
                h, k, vt = _ffn_kv(h, g1, wgu, wd, (layer, 1), kv_norm_g.reshape(1, d),
                                   w_kv[:, :K_WIDTH].astype(BF16),
                                   w_kv[:, K_WIDTH:].T.astype(BF16), batch, seq)
        else:
            j = layer - N_A_LAYERS
            lambda_init = 0.8 - 0.6 * math.exp(-0.3 * layer)
            h, qt = _ffn_q(h, g0, wgu, wd, (layer, 0), g_mix, b_w_q[j].T.astype(BF16),
                           batch, seq)
            ot = _attn(qt, k, vt, b_lambda[j], b_subln_g[j].reshape(-1, 1), lambda_init,
                       batch, seq)
            h = _oproj_ffn(h, ot, b_w_o[j].astype(BF16), g1, wgu, wd, fg, (layer, 1), seq,
                           final_norm=(layer == DEPTH - 1))
    return h.reshape(batch, seq, d)
```
